```python
import math, functools
import jax, jax.numpy as jnp
from jax import lax
import numpy as np

D_MODEL = 2048
BATCH = 4
SEQ = 2048
DEPTH = 1
DEC_BATCH = 128
DEC_SEQ = 1
PAST_LEN = 8192
PAGE_SIZE = 128

D_POOL = D_MODEL // 2
POOL_WINDOWS = (2, 4, 8, 16)
N_POOL_GROUPS = len(POOL_WINDOWS)
POOL_GROUP = D_POOL // N_POOL_GROUPS
POOL_STATE = max(POOL_WINDOWS) - 1
QK_NOPE = 128
QK_ROPE = 64
V_HEAD = 128
N_HEADS = (D_MODEL // 2) // V_HEAD
Q_LORA = 512
KV_LORA = 512
D_ATTN = N_HEADS * V_HEAD
D_MIX = D_POOL + D_ATTN
D_IN = D_POOL + Q_LORA + KV_LORA + QK_ROPE
D_FF = 5632
ROPE_THETA = 10000.0
Q_BLOCK = 128
EPS = 1e-6
SM_SCALE = (QK_NOPE + QK_ROPE) ** -0.5
POOL_SPARE_NUM = 5
POOL_SPARE_DEN = 4

kernel_name = 'hybrid_pool_mla_macaron_step'


def rms_norm(x, g):
    xf = x.astype(jnp.float32)
    y = xf * lax.rsqrt(jnp.mean(xf * xf, axis=-1, keepdims=True) + EPS)
    return (y * g.astype(jnp.float32)).astype(x.dtype)


def swiglu(x, w_gate, w_up, w_down):
    return (jax.nn.silu(x @ w_gate) * (x @ w_up)) @ w_down


def rope(x, pos):
    half = QK_ROPE // 2
    inv = ROPE_THETA ** (-jnp.arange(half, dtype=jnp.float32) / half)
    ang = pos.astype(jnp.float32)[:, None] * inv[None, :]
    cos = jnp.cos(ang)[None, :, None, :]
    sin = jnp.sin(ang)[None, :, None, :]
    xf = x.astype(jnp.float32)
    x1, x2 = xf[..., :half], xf[..., half:]
    return jnp.concatenate([x1 * cos - x2 * sin, x2 * cos + x1 * sin], axis=-1).astype(x.dtype)


def multiscale_pool(u_hist, u_new, pos_new, w_pool, pool_scale):
    B, T, _ = u_new.shape
    u_ext = jnp.concatenate([u_hist, u_new], axis=1).astype(jnp.float32)
    cs = jnp.cumsum(u_ext, axis=1)
    cs = jnp.concatenate([jnp.zeros_like(cs[:, :1]), cs], axis=1)
    end = cs[:, POOL_STATE + 1:]
    means = []
    for g, w in enumerate(POOL_WINDOWS):
        sl = slice(g * POOL_GROUP, (g + 1) * POOL_GROUP)
        start = cs[:, POOL_STATE + 1 - w:POOL_STATE + 1 - w + T, sl]
        cnt = jnp.minimum(pos_new + 1, w).astype(jnp.float32)[None, :, None]
        means.append((end[..., sl] - start) / cnt)
    mean = jnp.concatenate(means, axis=-1)
    d = (mean - u_ext[:, POOL_STATE:]).reshape(B, T, N_POOL_GROUPS, POOL_GROUP)
    out = jnp.einsum('btgc,gcd->btgd', d.astype(w_pool.dtype), w_pool).reshape(B, T, D_POOL)
    return (out * pool_scale).astype(u_new.dtype)


def attend_latent(q_lat, q_pe, c_kv, k_pe, mask):
    s = (jnp.einsum('bqhc,bsc->bhqs', q_lat, c_kv).astype(jnp.float32)
         + jnp.einsum('bqhr,bsr->bhqs', q_pe, k_pe).astype(jnp.float32)) * SM_SCALE
    s = jnp.where(mask[None, None], s, -jnp.inf)
    p = jax.nn.softmax(s, axis=-1).astype(c_kv.dtype)
    return jnp.einsum('bhqs,bsc->bqhc', p, c_kv)


def prompt_attend(q_lat, q_pe, c_kv, k_pe):
    B, S = q_lat.shape[:2]
    nb = S // Q_BLOCK
    key_pos = jnp.arange(S)

    def block(args):
        ql, qp, i = args
        qpos = i * Q_BLOCK + jnp.arange(Q_BLOCK)
        mask = key_pos[None, :] <= qpos[:, None]
        return attend_latent(ql, qp, c_kv, k_pe, mask)

    qlb = q_lat.reshape(B, nb, Q_BLOCK, N_HEADS, KV_LORA).swapaxes(0, 1)
    qpb = q_pe.reshape(B, nb, Q_BLOCK, N_HEADS, QK_ROPE).swapaxes(0, 1)
    o = lax.map(block, (qlb, qpb, jnp.arange(nb)))
    return o.swapaxes(0, 1).reshape(B, S, N_HEADS, KV_LORA)


def sample_attend(q_lat, q_pe, c_kv, k_pe, ckv_past, kpe_past):
    T = q_lat.shape[1]
    P = ckv_past.shape[1]
    c_all = jnp.concatenate([ckv_past.astype(c_kv.dtype), c_kv], axis=1)
    k_all = jnp.concatenate([kpe_past.astype(k_pe.dtype), k_pe], axis=1)
    mask = jnp.arange(P + T)[None, :] <= (P + jnp.arange(T))[:, None]
    return attend_latent(q_lat, q_pe, c_all, k_all, mask)


def decoder_layer(x, pos, pool_hist, attend, lp):
    (g_ffn1_pre, w1_gate, w1_up, w1_down, g_ffn1_post,
     g_mix_pre, w_in, w_pool, pool_scale, g_q, w_uq, g_kv, w_uk, w_uv, w_out, g_mix_post,
     g_ffn2_pre, w2_gate, w2_up, w2_down, g_ffn2_post) = lp
    B, T, _ = x.shape
    h = x
    h = h + 0.5 * rms_norm(swiglu(rms_norm(h, g_ffn1_pre), w1_gate, w1_up, w1_down), g_ffn1_post)
    z = rms_norm(h, g_mix_pre) @ w_in
    u = z[..., :D_POOL]
    c_q = z[..., D_POOL:D_POOL + Q_LORA]
    c_kv_raw = z[..., D_POOL + Q_LORA:D_POOL + Q_LORA + KV_LORA]
    k_pe_raw = z[..., D_POOL + Q_LORA + KV_LORA:]
    pool_out = multiscale_pool(pool_hist, u, pos, w_pool, pool_scale)
    q = (rms_norm(c_q, g_q) @ w_uq).reshape(B, T, N_HEADS, QK_NOPE + QK_ROPE)
    q_nope, q_pe = q[..., :QK_NOPE], rope(q[..., QK_NOPE:], pos)
    c_kv = rms_norm(c_kv_raw, g_kv)
    k_pe = rope(k_pe_raw[:, :, None, :], pos)[:, :, 0]
    q_lat = jnp.einsum('bqhd,chd->bqhc', q_nope, w_uk)
    o_lat = attend(q_lat, q_pe, c_kv, k_pe)
    attn_out = jnp.einsum('bqhc,chv->bqhv', o_lat, w_uv).reshape(B, T, D_ATTN)
    mix = jnp.concatenate([pool_out, attn_out], axis=-1) @ w_out
    h = h + rms_norm(mix, g_mix_post)
    h = h + 0.5 * rms_norm(swiglu(rms_norm(h, g_ffn2_pre), w2_gate, w2_up, w2_down), g_ffn2_post)
    new_pool = jnp.concatenate([pool_hist.astype(u.dtype), u], axis=1)[:, -POOL_STATE:]
    return h, c_kv, k_pe, new_pool


def setup_inputs(seed: int = 0) -> dict:
    key = jax.random.key(seed)
    ks = iter(jax.random.split(key, 40))
    f32 = jnp.float32

    def nrm(shape, fan_in):
        return jax.random.normal(next(ks), shape, f32) * (fan_in ** -0.5)

    def gain(n):
        return 1.0 + 0.02 * jax.random.normal(next(ks), (DEPTH, n), f32)

    n_pages = PAST_LEN // PAGE_SIZE
    n_used = DEC_BATCH * n_pages
    n_phys = n_used * POOL_SPARE_NUM // POOL_SPARE_DEN
    perm = jax.random.permutation(next(ks), n_phys)
    page_table = perm[:n_used].reshape(DEC_BATCH, n_pages).astype(jnp.int32)
    return {
        'x_prompt': jax.random.normal(next(ks), (BATCH, SEQ, D_MODEL), f32),
        'x_sample': jax.random.normal(next(ks), (DEC_BATCH, DEC_SEQ, D_MODEL), f32),
        'cache_kv_latent': jax.random.normal(next(ks), (DEPTH, n_phys, PAGE_SIZE, KV_LORA), f32),
        'cache_k_rope': jax.random.normal(next(ks), (DEPTH, n_phys, PAGE_SIZE, QK_ROPE), f32),
        'state_pool': jax.random.normal(next(ks), (DEPTH, DEC_BATCH, POOL_STATE, D_POOL), f32),
        'page_table': page_table,
        'g_ffn1_pre': gain(D_MODEL),
        'w1_gate': nrm((DEPTH, D_MODEL, D_FF), D_MODEL),
        'w1_up': nrm((DEPTH, D_MODEL, D_FF), D_MODEL),
        'w1_down': nrm((DEPTH, D_FF, D_MODEL), D_FF),
        'g_ffn1_post': gain(D_MODEL),
        'g_mix_pre': gain(D_MODEL),
        'w_in': nrm((DEPTH, D_MODEL, D_IN), D_MODEL),
        'w_pool': nrm((DEPTH, N_POOL_GROUPS, POOL_GROUP, POOL_GROUP), POOL_GROUP),
        'pool_scale': gain(D_POOL),
        'g_q': gain(Q_LORA),
        'w_uq': nrm((DEPTH, Q_LORA, N_HEADS * (QK_NOPE + QK_ROPE)), Q_LORA),
        'g_kv': gain(KV_LORA),
        'w_uk': nrm((DEPTH, KV_LORA, N_HEADS, QK_NOPE), KV_LORA),
        'w_uv': nrm((DEPTH, KV_LORA, N_HEADS, V_HEAD), KV_LORA),
        'w_out': nrm((DEPTH, D_MIX, D_MODEL), D_MIX),
        'g_mix_post': gain(D_MODEL),
        'g_ffn2_pre': gain(D_MODEL),
        'w2_gate': nrm((DEPTH, D_MODEL, D_FF), D_MODEL),
        'w2_up': nrm((DEPTH, D_MODEL, D_FF), D_MODEL),
        'w2_down': nrm((DEPTH, D_FF, D_MODEL), D_FF),
        'g_ffn2_post': gain(D_MODEL),
    }


def reference(x_prompt, x_sample, cache_kv_latent, cache_k_rope, state_pool, page_table,
              g_ffn1_pre, w1_gate, w1_up, w1_down, g_ffn1_post,
              g_mix_pre, w_in, w_pool, pool_scale, g_q, w_uq, g_kv, w_uk, w_uv, w_out, g_mix_post,
              g_ffn2_pre, w2_gate, w2_up, w2_down, g_ffn2_post):
    n_pages = page_table.shape[1]
    past_len = n_pages * PAGE_SIZE
    bp, sp = x_prompt.shape[0], x_prompt.shape[1]
    bs, ts = x_sample.shape[0], x_sample.shape[1]
    pos_p = jnp.arange(sp, dtype=jnp.int32)
    pos_s = past_len + jnp.arange(ts, dtype=jnp.int32)
    hp, hs = x_prompt, x_sample
    p_kv, p_pe, p_pool, s_kv, s_pe, s_pool = [], [], [], [], [], []
    for l in range(DEPTH):
        lp = (g_ffn1_pre[l], w1_gate[l], w1_up[l], w1_down[l], g_ffn1_post[l],
              g_mix_pre[l], w_in[l], w_pool[l], pool_scale[l], g_q[l], w_uq[l], g_kv[l],
              w_uk[l], w_uv[l], w_out[l], g_mix_post[l],
              g_ffn2_pre[l], w2_gate[l], w2_up[l], w2_down[l], g_ffn2_post[l])
        hist0 = jnp.zeros((bp, POOL_STATE, D_POOL), hp.dtype)
        hp, ckv, kpe, pool = decoder_layer(hp, pos_p, hist0, prompt_attend, lp)
        p_kv.append(ckv)
        p_pe.append(kpe)
        p_pool.append(pool)
        ckv_past = cache_kv_latent[l][page_table].reshape(bs, past_len, KV_LORA)
        kpe_past = cache_k_rope[l][page_table].reshape(bs, past_len, QK_ROPE)
        attend = functools.partial(sample_attend, ckv_past=ckv_past, kpe_past=kpe_past)
        hs, ckv_s, kpe_s, pool_s = decoder_layer(hs, pos_s, state_pool[l], attend, lp)
        s_kv.append(ckv_s)
        s_pe.append(kpe_s)
        s_pool.append(pool_s)
    return (hp, hs, jnp.stack(p_kv), jnp.stack(p_pe), jnp.stack(p_pool),
            jnp.stack(s_kv), jnp.stack(s_pe), jnp.stack(s_pool))
```

```python
import functools
import math

import jax
import jax.numpy as jnp
from jax import lax
from jax.experimental import pallas as pl
from jax.experimental.pallas import tpu as pltpu

F32 = jnp.float32
BF16 = jnp.bfloat16

EPS = 1e-6
ROPE_THETA = 10000.0
POOL_WINDOWS = (2, 4, 8, 16)
POOL_STATE = max(POOL_WINDOWS) - 1
QK_NOPE = 128
QK_ROPE = 64
V_HEAD = 128
HEAD_W = 2 * 128

V7X_VMEM_BYTES = 64 * 1024 * 1024
VMEM_LIMIT = V7X_VMEM_BYTES - 8 * 1024 * 1024

ROW_TILE = 512
FF_TILE = 512
PROJ_TILE = 256
POOL_TILE = 512
ATTN_TILE = 512
DECODE_PAGES = 16


def _params(n_grid_dims):
    return pltpu.CompilerParams(
        dimension_semantics=("arbitrary",) * n_grid_dims, vmem_limit_bytes=VMEM_LIMIT)


def _rms(x, g):
    return x * lax.rsqrt(jnp.mean(x * x, axis=-1, keepdims=True) + EPS) * g


def _dot(a, b):
    return jnp.dot(a, b, preferred_element_type=F32)


def _dot_nt(a, b):
    return lax.dot_general(a, b, (((1,), (1,)), ((), ())), preferred_element_type=F32)


def _ffn_kernel(xp_ref, xs_ref, gpre_ref, wg_ref, wu_ref, wd_ref, gpost_ref,
                yp_ref, ys_ref, xn_ref, acc_ref, *, n_prompt_tiles, n_ff_tiles, sample_rows):
    i = pl.program_id(0)
    j = pl.program_id(1)

    def body(x_ref, y_ref, rows):
        @pl.when(j == 0)
        def _():
            xn_ref[:rows, :] = _rms(x_ref[...], gpre_ref[...]).astype(BF16)

        xn = xn_ref[:rows, :]
        gate = _dot(xn, wg_ref[...])
        up = _dot(xn, wu_ref[...])
        hidden = (gate / (1.0 + jnp.exp(-gate))) * up
        part = _dot(hidden.astype(BF16), wd_ref[...])

        @pl.when(j == 0)
        def _():
            acc_ref[:rows, :] = part

        @pl.when(j > 0)
        def _():
            acc_ref[:rows, :] += part

        @pl.when(j == n_ff_tiles - 1)
        def _():
            y_ref[...] = x_ref[...] + 0.5 * _rms(acc_ref[:rows, :], gpost_ref[...])

    @pl.when(i < n_prompt_tiles)
    def _():
        body(xp_ref, yp_ref, ROW_TILE)

    @pl.when(i == n_prompt_tiles)
    def _():
        body(xs_ref, ys_ref, sample_rows)


def _ffn(xp, xs, g_pre, w_gate, w_up, w_down, g_post):
    mp, d = xp.shape
    ms = xs.shape[0]
    d_ff = w_gate.shape[1]
    ntp = mp // ROW_TILE
    nj = d_ff // FF_TILE
    assert mp % ROW_TILE == 0 and d_ff % FF_TILE == 0 and ms <= ROW_TILE
    prompt_row = lambda i, j: (jnp.minimum(i, ntp - 1), 0)
    fixed = lambda i, j: (0, 0)
    return pl.pallas_call(
        functools.partial(_ffn_kernel, n_prompt_tiles=ntp, n_ff_tiles=nj, sample_rows=ms),
        out_shape=(jax.ShapeDtypeStruct((mp, d), F32), jax.ShapeDtypeStruct((ms, d), F32)),
        grid=(ntp + 1, nj),
        in_specs=[
            pl.BlockSpec((ROW_TILE, d), prompt_row),
            pl.BlockSpec((ms, d), fixed),
            pl.BlockSpec((1, d), fixed),
            pl.BlockSpec((d, FF_TILE), lambda i, j: (0, j)),
            pl.BlockSpec((d, FF_TILE), lambda i, j: (0, j)),
            pl.BlockSpec((FF_TILE, d), lambda i, j: (j, 0)),
            pl.BlockSpec((1, d), fixed),
        ],
        out_specs=(pl.BlockSpec((ROW_TILE, d), prompt_row), pl.BlockSpec((ms, d), fixed)),
        scratch_shapes=[pltpu.VMEM((ROW_TILE, d), BF16), pltpu.VMEM((ROW_TILE, d), F32)],
        compiler_params=_params(2),
        name="ffn",
    )(xp, xs, g_pre, w_gate, w_up, w_down, g_post)


def _rope_pairs(slab, cs):
    t = slab * cs
    return t + pltpu.roll(t, QK_ROPE, axis=1)


def _proj_kernel(hp_ref, hs_ref, csp_ref, css_ref, gmix_ref, win_ref, gq_ref, wq_ref, gkv_ref,
                 wkv_ref, wukt_ref,
                 up_ref, ckvp_ref, kpep_ref, q_ref, k_ref, v_ref,
                 us_ref, ckvs_ref, kpes_ref, qlat_ref, qpes_ref,
                 *, n_prompt_tiles, n_heads, d_pool, q_lora, kv_lora):
    i = pl.program_id(0)

    def common(h_ref, cs_ref, u_ref, ckv_ref, kpe_ref):
        hn = _rms(h_ref[...], gmix_ref[...]).astype(BF16)
        z = _dot(hn, win_ref[...])
        u_ref[...] = z[:, :d_pool]
        cqn = _rms(z[:, d_pool:d_pool + q_lora], gq_ref[...]).astype(BF16)
        ckv = _rms(z[:, d_pool + q_lora:d_pool + q_lora + kv_lora], gkv_ref[...])
        ckv_ref[...] = ckv
        cs = cs_ref[...]
        kpe = _rope_pairs(z[:, d_pool + q_lora + kv_lora:], cs)
        kpe_ref[...] = kpe[:, :QK_ROPE]
        qq = _dot(cqn, wq_ref[...])
        return ckv, kpe, qq, cs

    @pl.when(i < n_prompt_tiles)
    def _():
        ckv, kpe, qq, cs = common(hp_ref, csp_ref, up_ref, ckvp_ref, kpep_ref)
        lane = lax.broadcasted_iota(jnp.int32, kpe.shape, 1)
        kpe_pad = jnp.where(lane < QK_ROPE, kpe, 0.0).astype(BF16)
        kv = _dot(ckv.astype(BF16), wkv_ref[...])
        v_ref[...] = kv[:, n_heads * QK_NOPE:].astype(BF16)
        for h in range(n_heads):
            slab = qq[:, h * HEAD_W:(h + 1) * HEAD_W]
            q_ref[h, :, :QK_NOPE] = slab[:, :QK_NOPE].astype(BF16)
            q_ref[h, :, QK_NOPE:] = _rope_pairs(slab[:, QK_NOPE:], cs).astype(BF16)
            k_ref[h, :, :QK_NOPE] = kv[:, h * QK_NOPE:(h + 1) * QK_NOPE].astype(BF16)
            k_ref[h, :, QK_NOPE:] = kpe_pad

    @pl.when(i == n_prompt_tiles)
    def _():
        _, _, qq, cs = common(hs_ref, css_ref, us_ref, ckvs_ref, kpes_ref)
        for h in range(n_heads):
            slab = qq[:, h * HEAD_W:(h + 1) * HEAD_W]
            qlat_ref[h] = _dot(slab[:, :QK_NOPE].astype(BF16), wukt_ref[h]).astype(BF16)
            qpes_ref[h] = _rope_pairs(slab[:, QK_NOPE:], cs)[:, :QK_ROPE].astype(BF16)


def _proj(hp, hs, cs_p, cs_s, g_mix, w_in2, g_q, w_q2, g_kv, w_kv2, w_uk_t, *, seq, n_heads):
    mp, d = hp.shape
    ms = hs.shape[0]
    tm = PROJ_TILE
    ntp = mp // tm
    tiles_per_seq = seq // tm
    assert mp % tm == 0 and seq % tm == 0
    d_in2 = w_in2.shape[1]
    q_lora = w_q2.shape[0]
    kv_lora = w_kv2.shape[0]
    d_pool = d_in2 - q_lora - kv_lora - 2 * QK_ROPE
    prompt_row = lambda i: (jnp.minimum(i, ntp - 1), 0)
    prompt_head_row = lambda i: (0, jnp.minimum(i, ntp - 1), 0)
    fixed2 = lambda i: (0, 0)
    fixed3 = lambda i: (0, 0, 0)
    whole = lambda a: pl.BlockSpec(a.shape, fixed2 if a.ndim == 2 else fixed3)
    out_shape = (
        jax.ShapeDtypeStruct((mp, d_pool), F32),
        jax.ShapeDtypeStruct((mp, kv_lora), F32),
        jax.ShapeDtypeStruct((mp, QK_ROPE), F32),
        jax.ShapeDtypeStruct((n_heads, mp, HEAD_W), BF16),
        jax.ShapeDtypeStruct((n_heads, mp, HEAD_W), BF16),
        jax.ShapeDtypeStruct((mp, n_heads * V_HEAD), BF16),
        jax.ShapeDtypeStruct((ms, d_pool), F32),
        jax.ShapeDtypeStruct((ms, kv_lora), F32),
        jax.ShapeDtypeStruct((ms, QK_ROPE), F32),
        jax.ShapeDtypeStruct((n_heads, ms, kv_lora), BF16),
        jax.ShapeDtypeStruct((n_heads, ms, QK_ROPE), BF16),
    )
    out_specs = (
        pl.BlockSpec((tm, d_pool), prompt_row),
        pl.BlockSpec((tm, kv_lora), prompt_row),
        pl.BlockSpec((tm, QK_ROPE), prompt_row),
        pl.BlockSpec((n_heads, tm, HEAD_W), prompt_head_row),
        pl.BlockSpec((n_heads, tm, HEAD_W), prompt_head_row),
        pl.BlockSpec((tm, n_heads * V_HEAD), prompt_row),
        pl.BlockSpec((ms, d_pool), fixed2),
        pl.BlockSpec((ms, kv_lora), fixed2),
        pl.BlockSpec((ms, QK_ROPE), fixed2),
        pl.BlockSpec((n_heads, ms, kv_lora), fixed3),
        pl.BlockSpec((n_heads, ms, QK_ROPE), fixed3),
    )
    return pl.pallas_call(
        functools.partial(_proj_kernel, n_prompt_tiles=ntp, n_heads=n_heads, d_pool=d_pool,
                          q_lora=q_lora, kv_lora=kv_lora),
        out_shape=out_shape,
        grid=(ntp + 1,),
        in_specs=[
            pl.BlockSpec((tm, d), prompt_row),
            pl.BlockSpec((ms, d), fixed2),
            pl.BlockSpec((tm, 2 * QK_ROPE), lambda i: (jnp.minimum(i, ntp - 1) % tiles_per_seq, 0)),
            whole(cs_s), whole(g_mix), whole(w_in2), whole(g_q), whole(w_q2), whole(g_kv),
            whole(w_kv2), whole(w_uk_t),
        ],
        out_specs=out_specs,
        compiler_params=_params(1),
        name="proj",
    )(hp, hs, cs_p, cs_s, g_mix, w_in2, g_q, w_q2, g_kv, w_kv2, w_uk_t)


def _pool_delta(window_sum, cur, inv_cnt):
    return window_sum * inv_cnt - cur


def _pool_kernel(u_ref, wp_ref, scale_ref, o_ref, ext_ref, *, group):
    i = pl.program_id(1)
    ts = POOL_TILE
    halo = POOL_STATE + 1

    @pl.when(i == 0)
    def _():
        ext_ref[:halo, :] = jnp.zeros((halo, ext_ref.shape[1]), F32)

    @pl.when(i > 0)
    def _():
        ext_ref[:halo, :] = ext_ref[ts:ts + halo, :]

    ext_ref[halo:, :] = u_ref[...]
    pos = i * ts + lax.broadcasted_iota(jnp.int32, (ts, 1), 0)
    for g, w in enumerate(POOL_WINDOWS):
        cols = slice(g * group, (g + 1) * group)
        cur = ext_ref[halo:, cols]
        acc = cur
        for k in range(1, w):
            acc = acc + ext_ref[halo - k:halo - k + ts, cols]
        inv_cnt = 1.0 / jnp.minimum(pos + 1, w).astype(F32)
        delta = _pool_delta(acc, cur, inv_cnt)
        out = _dot(delta.astype(BF16), wp_ref[g])
        o_ref[:, cols] = (out * scale_ref[:, cols]).astype(BF16)


def _pool(u, w_pool, pool_scale, *, batch, seq):
    m, d_pool = u.shape
    ts = POOL_TILE
    nt = seq // ts
    group = d_pool // len(POOL_WINDOWS)
    assert seq % ts == 0 and m == batch * seq
    return pl.pallas_call(
        functools.partial(_pool_kernel, group=group),
        out_shape=jax.ShapeDtypeStruct((m, d_pool), BF16),
        grid=(batch, nt),
        in_specs=[
            pl.BlockSpec((ts, d_pool), lambda b, i: (b * nt + i, 0)),
            pl.BlockSpec(w_pool.shape, lambda b, i: (0, 0, 0)),
            pl.BlockSpec((1, d_pool), lambda b, i: (0, 0)),
        ],
        out_specs=pl.BlockSpec((ts, d_pool), lambda b, i: (b * nt + i, 0)),
        scratch_shapes=[pltpu.VMEM((ts + POOL_STATE + 1, d_pool), F32)],
        compiler_params=_params(2),
        name="pool",
    )(u, w_pool, pool_scale)


def _attn_kernel(q_ref, k_ref, v_ref, o_ref, *, n_heads, sm_scale):
    qi = pl.program_id(1)
    t = ATTN_TILE
    row = lax.broadcasted_iota(jnp.int32, (t, t), 0)
    col = lax.broadcasted_iota(jnp.int32, (t, t), 1)
    causal = row >= col

    for h in range(n_heads):
        q = q_ref[h]

        def block(j, carry, diagonal):
            m, l, acc = carry
            start = pl.multiple_of(j * t, t)
            k = k_ref[h, pl.ds(start, t), :]
            v = v_ref[pl.ds(start, t), h * V_HEAD:(h + 1) * V_HEAD]
            s = _dot_nt(q, k) * sm_scale
            if diagonal:
                s = jnp.where(causal, s, -jnp.inf)
            m_new = jnp.maximum(m, jnp.max(s, axis=1, keepdims=True))
            alpha = jnp.exp(m - m_new)
            p = jnp.exp(s - m_new)
            l = alpha * l + jnp.sum(p, axis=1, keepdims=True)
            acc = alpha * acc + _dot(p.astype(BF16), v)
            return m_new, l, acc

        init = (jnp.full((t, 1), -jnp.inf, F32), jnp.zeros((t, 1), F32), jnp.zeros((t, V_HEAD), F32))
        carry = lax.fori_loop(0, qi, lambda j, c: block(j, c, False), init)
        _, l, acc = block(qi, carry, True)
        o_ref[:, h * V_HEAD:(h + 1) * V_HEAD] = (acc / l).astype(BF16)


def _attn(q, k, v, *, batch, seq, sm_scale):
    n_heads, m, _ = q.shape
    t = ATTN_TILE
    nq = seq // t
    assert seq % t == 0
    return pl.pallas_call(
        functools.partial(_attn_kernel, n_heads=n_heads, sm_scale=sm_scale),
        out_shape=jax.ShapeDtypeStruct((m, n_heads * V_HEAD), BF16),
        grid=(batch, nq),
        in_specs=[
            pl.BlockSpec((n_heads, t, HEAD_W), lambda b, i: (0, b * nq + i, 0)),
            pl.BlockSpec((n_heads, seq, HEAD_W), lambda b, i: (0, b, 0)),
            pl.BlockSpec((seq, n_heads * V_HEAD), lambda b, i: (b, 0)),
        ],
        out_specs=pl.BlockSpec((t, n_heads * V_HEAD), lambda b, i: (b * nq + i, 0)),
        compiler_params=_params(2),
        name="attn",
    )(q, k, v)


def _decode_kernel(pt_ref, qlat_ref, qpe_ref, ckvs_ref, kpes_ref, *rest, n_steps, sm_scale):
    del pt_ref
    n = DECODE_PAGES
    kv_refs, rope_refs = rest[:n], rest[n:2 * n]
    o_ref, m_ref, l_ref, acc_ref = rest[2 * n:]
    g = pl.program_id(1)

    @pl.when(g == 0)
    def _():
        m_ref[...] = jnp.full(m_ref.shape, -jnp.inf, F32)
        l_ref[...] = jnp.zeros(l_ref.shape, F32)
        acc_ref[...] = jnp.zeros(acc_ref.shape, F32)

    qlat = qlat_ref[0]
    qpe = qpe_ref[0]
    ckv = jnp.concatenate([r[...].astype(BF16) for r in kv_refs], axis=0)
    kpe = jnp.concatenate([r[...].astype(BF16) for r in rope_refs], axis=0)
    s = (_dot_nt(qlat, ckv) + _dot_nt(qpe, kpe)) * sm_scale
    m_old = m_ref[...]
    m_new = jnp.maximum(m_old, jnp.max(s, axis=1, keepdims=True))
    alpha = jnp.exp(m_old - m_new)
    p = jnp.exp(s - m_new)
    l_new = alpha * l_ref[...] + jnp.sum(p, axis=1, keepdims=True)
    acc_new = alpha * acc_ref[...] + _dot(p.astype(BF16), ckv)
    m_ref[...] = m_new
    l_ref[...] = l_new
    acc_ref[...] = acc_new

    @pl.when(g == n_steps - 1)
    def _():
        own_ckv = ckvs_ref[0].astype(BF16).astype(F32)
        own_kpe = kpes_ref[0].astype(BF16).astype(F32)
        s_own = (jnp.sum(qlat.astype(F32) * own_ckv, axis=1, keepdims=True)
                 + jnp.sum(qpe.astype(F32) * own_kpe, axis=1, keepdims=True)) * sm_scale
        m_fin = jnp.maximum(m_new, s_own)
        a = jnp.exp(m_new - m_fin)
        p_own = jnp.exp(s_own - m_fin)
        l_fin = a * l_new + p_own
        acc_fin = a * acc_new + p_own.astype(BF16).astype(F32) * own_ckv
        o_ref[0] = acc_fin / l_fin


def _decode(page_table, q_lat, q_pe, ckv_own, kpe_own, cache_kv, cache_rope, *, sm_scale):
    bs, n_heads, kv_lora = q_lat.shape
    n_pages = page_table.shape[1]
    page = cache_kv.shape[1]
    n = DECODE_PAGES
    n_steps = n_pages // n
    assert n_pages % n == 0

    def page_map(k):
        return lambda b, g, pt: (pt[b * n_pages + g * n + k], 0, 0)

    per_sample = lambda b, g, pt: (b, 0, 0)
    grid_spec = pltpu.PrefetchScalarGridSpec(
        num_scalar_prefetch=1,
        grid=(bs, n_steps),
        in_specs=[
            pl.BlockSpec((1, n_heads, kv_lora), per_sample),
            pl.BlockSpec((1, n_heads, QK_ROPE), per_sample),
            pl.BlockSpec((1, 1, kv_lora), per_sample),
            pl.BlockSpec((1, 1, QK_ROPE), per_sample),
            *[pl.BlockSpec((None, page, kv_lora), page_map(k)) for k in range(n)],
            *[pl.BlockSpec((None, page, QK_ROPE), page_map(k)) for k in range(n)],
        ],
        out_specs=pl.BlockSpec((1, n_heads, kv_lora), per_sample),
        scratch_shapes=[pltpu.VMEM((n_heads, 1), F32), pltpu.VMEM((n_heads, 1), F32),
                        pltpu.VMEM((n_heads, kv_lora), F32)],
    )
    return pl.pallas_call(
        functools.partial(_decode_kernel, n_steps=n_steps, sm_scale=sm_scale),
        out_shape=jax.ShapeDtypeStruct((bs, n_heads, kv_lora), F32),
        grid_spec=grid_spec,
        compiler_params=_params(2),
        name="decode",
    )(page_table.reshape(-1), q_lat, q_pe, ckv_own, kpe_own,
      *([cache_kv] * n), *([cache_rope] * n))


def _sample_kernel(olat_ref, wuv_ref, sp_ref, us_ref, wp_ref, scale_ref, pool_ref, attn_ref,
                   *, n_heads, group, past_len):
    for h in range(n_heads):
        attn_ref[:, h * V_HEAD:(h + 1) * V_HEAD] = _dot(olat_ref[h].astype(BF16), wuv_ref[h]).astype(BF16)
    for g, w in enumerate(POOL_WINDOWS):
        cols = slice(g * group, (g + 1) * group)
        cur = us_ref[:, cols]
        hist = sp_ref[:, POOL_STATE - (w - 1):, cols]
        acc = cur + jnp.sum(hist, axis=1)
        delta = _pool_delta(acc, cur, 1.0 / min(past_len + 1, w))
        out = _dot(delta.astype(BF16), wp_ref[g])
        pool_ref[:, cols] = (out * scale_ref[:, cols]).astype(BF16)


def _sample_mix(o_lat, w_uv_h, state_pool, u_s, w_pool, pool_scale, *, past_len):
    n_heads, ms, _ = o_lat.shape
    d_pool = u_s.shape[1]
    return pl.pallas_call(
        functools.partial(_sample_kernel, n_heads=n_heads, group=d_pool // len(POOL_WINDOWS),
                          past_len=past_len),
        out_shape=(jax.ShapeDtypeStruct((ms, d_pool), BF16),
                   jax.ShapeDtypeStruct((ms, n_heads * V_HEAD), BF16)),
        compiler_params=pltpu.CompilerParams(vmem_limit_bytes=VMEM_LIMIT),
        name="sample_mix",
    )(o_lat, w_uv_h, state_pool, u_s, w_pool, pool_scale)


def _outproj_kernel(hp_ref, poolp_ref, attnp_ref, hs_ref, pools_ref, attns_ref, wo_ref, g_ref,
                    yp_ref, ys_ref, *, n_prompt_tiles, d_pool):
    i = pl.program_id(0)

    def body(h_ref, pool_ref, attn_ref, y_ref):
        mix = _dot(pool_ref[...], wo_ref[:d_pool, :]) + _dot(attn_ref[...], wo_ref[d_pool:, :])
        y_ref[...] = h_ref[...] + _rms(mix, g_ref[...])

    @pl.when(i < n_prompt_tiles)
    def _():
        body(hp_ref, poolp_ref, attnp_ref, yp_ref)

    @pl.when(i == n_prompt_tiles)
    def _():
        body(hs_ref, pools_ref, attns_ref, ys_ref)


def _outproj(hp, pool_p, attn_p, hs, pool_s, attn_s, w_out, g_post):
    mp, d = hp.shape
    ms = hs.shape[0]
    d_pool = pool_p.shape[1]
    d_attn = attn_p.shape[1]
    tm = ROW_TILE
    ntp = mp // tm
    prompt_row = lambda i: (jnp.minimum(i, ntp - 1), 0)
    fixed = lambda i: (0, 0)
    return pl.pallas_call(
        functools.partial(_outproj_kernel, n_prompt_tiles=ntp, d_pool=d_pool),
        out_shape=(jax.ShapeDtypeStruct((mp, d), F32), jax.ShapeDtypeStruct((ms, d), F32)),
        grid=(ntp + 1,),
        in_specs=[
            pl.BlockSpec((tm, d), prompt_row),
            pl.BlockSpec((tm, d_pool), prompt_row),
            pl.BlockSpec((tm, d_attn), prompt_row),
            pl.BlockSpec((ms, d), fixed),
            pl.BlockSpec((ms, d_pool), fixed),
            pl.BlockSpec((ms, d_attn), fixed),
            pl.BlockSpec(w_out.shape, fixed),
            pl.BlockSpec((1, d), fixed),
        ],
        out_specs=(pl.BlockSpec((tm, d), prompt_row), pl.BlockSpec((ms, d), fixed)),
        compiler_params=_params(1),
        name="outproj",
    )(hp, pool_p, attn_p, hs, pool_s, attn_s, w_out, g_post)


def _rotate_half_columns(w):
    half = QK_ROPE // 2
    return jnp.concatenate([-w[..., half:], w[..., :half]], axis=-1)


def _rope_table(pos):
    half = QK_ROPE // 2
    inv = ROPE_THETA ** (-jnp.arange(half, dtype=F32) / half)
    ang = pos.astype(F32)[:, None] * inv[None, :]
    cos, sin = jnp.cos(ang), jnp.sin(ang)
    return jnp.concatenate([cos, cos, sin, sin], axis=-1)


def kernel(x_prompt, x_sample, cache_kv_latent, cache_k_rope, state_pool, page_table,
           g_ffn1_pre, w1_gate, w1_up, w1_down, g_ffn1_post,
           g_mix_pre, w_in, w_pool, pool_scale, g_q, w_uq, g_kv, w_uk, w_uv, w_out, g_mix_post,
           g_ffn2_pre, w2_gate, w2_up, w2_down, g_ffn2_post):
    depth = w_in.shape[0]
    bp, sp, d = x_prompt.shape
    bs, ts, _ = x_sample.shape
    n_pages = page_table.shape[1]
    page = cache_kv_latent.shape[2]
    past_len = n_pages * page
    kv_lora, n_heads, _ = w_uk.shape[1:]
    q_lora = w_uq.shape[1]
    d_pool = w_pool.shape[1] * w_pool.shape[2]
    assert ts == 1, "the decode kernel handles one new token per sample sequence"
    sm_scale = (QK_NOPE + QK_ROPE) ** -0.5

    cs_p = _rope_table(jnp.arange(sp, dtype=jnp.int32))
    cs_s = jnp.tile(_rope_table(past_len + jnp.arange(ts, dtype=jnp.int32)), (bs, 1))

    hp = x_prompt.reshape(bp * sp, d)
    hs = x_sample.reshape(bs * ts, d)
    outs = [[] for _ in range(6)]
    for l in range(depth):
        row = lambda g: g[l][None, :]
        w_in_l = w_in[l]
        w_in2 = jnp.concatenate(
            [w_in_l, _rotate_half_columns(w_in_l[:, d_pool + q_lora + kv_lora:])], axis=1).astype(BF16)
        wq = w_uq[l].reshape(q_lora, n_heads, QK_NOPE + QK_ROPE)
        w_q2 = jnp.concatenate(
            [wq, _rotate_half_columns(wq[..., QK_NOPE:])], axis=-1).reshape(q_lora, n_heads * HEAD_W).astype(BF16)
        w_kv2 = jnp.concatenate(
            [w_uk[l].reshape(kv_lora, -1), w_uv[l].reshape(kv_lora, -1)], axis=1).astype(BF16)
        w_uk_t = jnp.transpose(w_uk[l], (1, 2, 0)).astype(BF16)
        w_uv_h = jnp.transpose(w_uv[l], (1, 0, 2)).astype(BF16)
        w_pool_l = w_pool[l].astype(BF16)
        w_out_l = w_out[l].astype(BF16)

        hp, hs = _ffn(hp, hs, row(g_ffn1_pre), w1_gate[l].astype(BF16), w1_up[l].astype(BF16),
                      w1_down[l].astype(BF16), row(g_ffn1_post))
        (u_p, ckv_p, kpe_p, q_p, k_p, v_p, u_s, ckv_s, kpe_s, qlat_s, qpe_s) = _proj(
            hp, hs, cs_p, cs_s, row(g_mix_pre), w_in2, row(g_q), w_q2, row(g_kv), w_kv2, w_uk_t,
            seq=sp, n_heads=n_heads)
        pool_p = _pool(u_p, w_pool_l, row(pool_scale), batch=bp, seq=sp)
        attn_p = _attn(q_p, k_p, v_p, batch=bp, seq=sp, sm_scale=sm_scale)
        o_lat = _decode(page_table, jnp.transpose(qlat_s, (1, 0, 2)), jnp.transpose(qpe_s, (1, 0, 2)),
                        ckv_s[:, None, :], kpe_s[:, None, :], cache_kv_latent[l], cache_k_rope[l],
                        sm_scale=sm_scale)
        pool_s, attn_s = _sample_mix(jnp.transpose(o_lat, (1, 0, 2)), w_uv_h, state_pool[l], u_s,
                                     w_pool_l, row(pool_scale), past_len=past_len)
        hp, hs = _outproj(hp, pool_p, attn_p, hs, pool_s, attn_s, w_out_l, row(g_mix_post))
        hp, hs = _ffn(hp, hs, row(g_ffn2_pre), w2_gate[l].astype(BF16), w2_up[l].astype(BF16),
                      w2_down[l].astype(BF16), row(g_ffn2_post))

        u_p3 = u_p.reshape(bp, sp, d_pool)
        outs[0].append(ckv_p.reshape(bp, sp, kv_lora))
        outs[1].append(kpe_p.reshape(bp, sp, QK_ROPE))
        outs[2].append(u_p3[:, sp - POOL_STATE:])
        outs[3].append(ckv_s.reshape(bs, ts, kv_lora))
        outs[4].append(kpe_s.reshape(bs, ts, QK_ROPE))
        outs[5].append(jnp.concatenate([state_pool[l][:, ts:], u_s.reshape(bs, ts, d_pool)], axis=1))
    return (hp.reshape(bp, sp, d), hs.reshape(bs, ts, d), *[jnp.stack(o) for o in outs])
```

```python
import functools
import math

import jax
import jax.numpy as jnp
from jax import lax
from jax.experimental import pallas as pl
from jax.experimental.pallas import tpu as pltpu

F32 = jnp.float32
BF16 = jnp.bfloat16

EPS = 1e-6
ROPE_THETA = 10000.0
POOL_WINDOWS = (2, 4, 8, 16)
POOL_STATE = max(POOL_WINDOWS) - 1
QK_NOPE = 128
QK_ROPE = 64
V_HEAD = 128
HEAD_W = 2 * 128

V7X_VMEM_BYTES = 64 * 1024 * 1024
VMEM_LIMIT = V7X_VMEM_BYTES - 8 * 1024 * 1024

ROW_TILE = 512
FF_TILE = 512
PROJ_TILE = 256
POOL_TILE = 512
ATTN_TILE = 512
DECODE_CHUNK_PAGES = 32


def _params(n_grid_dims):
    return pltpu.CompilerParams(
        dimension_semantics=("arbitrary",) * n_grid_dims, vmem_limit_bytes=VMEM_LIMIT)


def _rms(x, g):
    return x * lax.rsqrt(jnp.mean(x * x, axis=-1, keepdims=True) + EPS) * g


def _dot(a, b):
    return jnp.dot(a, b, preferred_element_type=F32)


def _dot_nt(a, b):
    return lax.dot_general(a, b, (((1,), (1,)), ((), ())), preferred_element_type=F32)


def _ffn_kernel(xp_ref, xs_ref, gpre_ref, wg_ref, wu_ref, wd_ref, gpost_ref,
                yp_ref, ys_ref, xn_ref, acc_ref, *, n_prompt_tiles, n_ff_tiles, sample_rows):
    i = pl.program_id(0)
    j = pl.program_id(1)

    def body(x_ref, y_ref, rows):
        @pl.when(j == 0)
        def _():
            xn_ref[:rows, :] = _rms(x_ref[...], gpre_ref[...]).astype(BF16)
            acc_ref[:rows, :] = jnp.zeros((rows, acc_ref.shape[1]), F32)

        xn = xn_ref[:rows, :]
        gate = _dot(xn, wg_ref[...])
        up = _dot(xn, wu_ref[...])
        hidden = (gate / (1.0 + jnp.exp(-gate))) * up
        acc_ref[:rows, :] += _dot(hidden.astype(BF16), wd_ref[...])

        @pl.when(j == n_ff_tiles - 1)
        def _():
            y_ref[...] = x_ref[...] + 0.5 * _rms(acc_ref[:rows, :], gpost_ref[...])

    @pl.when(i < n_prompt_tiles)
    def _():
        body(xp_ref, yp_ref, ROW_TILE)

    @pl.when(i == n_prompt_tiles)
    def _():
        body(xs_ref, ys_ref, sample_rows)


def _ffn(xp, xs, g_pre, w_gate, w_up, w_down, g_post):
    mp, d = xp.shape
    ms = xs.shape[0]
    d_ff = w_gate.shape[1]
    ntp = mp // ROW_TILE
    nj = d_ff // FF_TILE
    assert mp % ROW_TILE == 0 and d_ff % FF_TILE == 0 and ms <= ROW_TILE
    prompt_row = lambda i, j: (jnp.minimum(i, ntp - 1), 0)
    fixed = lambda i, j: (0, 0)
    return pl.pallas_call(
        functools.partial(_ffn_kernel, n_prompt_tiles=ntp, n_ff_tiles=nj, sample_rows=ms),
        out_shape=(jax.ShapeDtypeStruct((mp, d), F32), jax.ShapeDtypeStruct((ms, d), F32)),
        grid=(ntp + 1, nj),
        in_specs=[
            pl.BlockSpec((ROW_TILE, d), prompt_row),
            pl.BlockSpec((ms, d), fixed),
            pl.BlockSpec((1, d), fixed),
            pl.BlockSpec((d, FF_TILE), lambda i, j: (0, j)),
            pl.BlockSpec((d, FF_TILE), lambda i, j: (0, j)),
            pl.BlockSpec((FF_TILE, d), lambda i, j: (j, 0)),
            pl.BlockSpec((1, d), fixed),
        ],
        out_specs=(pl.BlockSpec((ROW_TILE, d), prompt_row), pl.BlockSpec((ms, d), fixed)),
        scratch_shapes=[pltpu.VMEM((ROW_TILE, d), BF16), pltpu.VMEM((ROW_TILE, d), F32)],
        compiler_params=_params(2),
        name="ffn",
    )(xp, xs, g_pre, w_gate, w_up, w_down, g_post)


def _rope_pairs(slab, cs):
    t = slab * cs
    return t + pltpu.roll(t, QK_ROPE, axis=1)


def _proj_kernel(hp_ref, hs_ref, csp_ref, css_ref, gmix_ref, win_ref, gq_ref, wq_ref, gkv_ref,
                 wkv_ref, wukt_ref,
                 up_ref, ckvp_ref, kpep_ref, q_ref, k_ref, v_ref,
                 us_ref, ckvs_ref, kpes_ref, qlat_ref, qpes_ref,
                 *, n_prompt_tiles, n_heads, d_pool, q_lora, kv_lora):
    i = pl.program_id(0)

    def common(h_ref, cs_ref, u_ref, ckv_ref, kpe_ref):
        hn = _rms(h_ref[...], gmix_ref[...]).astype(BF16)
        z = _dot(hn, win_ref[...])
        u_ref[...] = z[:, :d_pool]
        cqn = _rms(z[:, d_pool:d_pool + q_lora], gq_ref[...]).astype(BF16)
        ckv = _rms(z[:, d_pool + q_lora:d_pool + q_lora + kv_lora], gkv_ref[...])
        ckv_ref[...] = ckv
        cs = cs_ref[...]
        kpe = _rope_pairs(z[:, d_pool + q_lora + kv_lora:], cs)
        kpe_ref[...] = kpe[:, :QK_ROPE]
        qq = _dot(cqn, wq_ref[...])
        return ckv, kpe, qq, cs

    @pl.when(i < n_prompt_tiles)
    def _():
        ckv, kpe, qq, cs = common(hp_ref, csp_ref, up_ref, ckvp_ref, kpep_ref)
        lane = lax.broadcasted_iota(jnp.int32, kpe.shape, 1)
        kpe_pad = jnp.where(lane < QK_ROPE, kpe, 0.0).astype(BF16)
        kv = _dot(ckv.astype(BF16), wkv_ref[...])
        v_ref[...] = kv[:, n_heads * QK_NOPE:].astype(BF16)
        for h in range(n_heads):
            slab = qq[:, h * HEAD_W:(h + 1) * HEAD_W]
            q_ref[h, :, :QK_NOPE] = slab[:, :QK_NOPE].astype(BF16)
            q_ref[h, :, QK_NOPE:] = _rope_pairs(slab[:, QK_NOPE:], cs).astype(BF16)
            k_ref[h, :, :QK_NOPE] = kv[:, h * QK_NOPE:(h + 1) * QK_NOPE].astype(BF16)
            k_ref[h, :, QK_NOPE:] = kpe_pad

    @pl.when(i == n_prompt_tiles)
    def _():
        _, _, qq, cs = common(hs_ref, css_ref, us_ref, ckvs_ref, kpes_ref)
        for h in range(n_heads):
            slab = qq[:, h * HEAD_W:(h + 1) * HEAD_W]
            qlat_ref[h] = _dot(slab[:, :QK_NOPE].astype(BF16), wukt_ref[h]).astype(BF16)
            qpes_ref[h] = _rope_pairs(slab[:, QK_NOPE:], cs)[:, :QK_ROPE].astype(BF16)


def _proj(hp, hs, cs_p, cs_s, g_mix, w_in2, g_q, w_q2, g_kv, w_kv2, w_uk_t, *, seq, n_heads):
    mp, d = hp.shape
    ms = hs.shape[0]
    tm = PROJ_TILE
    ntp = mp // tm
    tiles_per_seq = seq // tm
    assert mp % tm == 0 and seq % tm == 0
    d_in2 = w_in2.shape[1]
    q_lora = w_q2.shape[0]
    kv_lora = w_kv2.shape[0]
    d_pool = d_in2 - q_lora - kv_lora - 2 * QK_ROPE
    prompt_row = lambda i: (jnp.minimum(i, ntp - 1), 0)
    prompt_head_row = lambda i: (0, jnp.minimum(i, ntp - 1), 0)
    fixed2 = lambda i: (0, 0)
    fixed3 = lambda i: (0, 0, 0)
    whole = lambda a: pl.BlockSpec(a.shape, fixed2 if a.ndim == 2 else fixed3)
    out_shape = (
        jax.ShapeDtypeStruct((mp, d_pool), F32),
        jax.ShapeDtypeStruct((mp, kv_lora), F32),
        jax.ShapeDtypeStruct((mp, QK_ROPE), F32),
        jax.ShapeDtypeStruct((n_heads, mp, HEAD_W), BF16),
        jax.ShapeDtypeStruct((n_heads, mp, HEAD_W), BF16),
        jax.ShapeDtypeStruct((mp, n_heads * V_HEAD), BF16),
        jax.ShapeDtypeStruct((ms, d_pool), F32),
        jax.ShapeDtypeStruct((ms, kv_lora), F32),
        jax.ShapeDtypeStruct((ms, QK_ROPE), F32),
        jax.ShapeDtypeStruct((n_heads, ms, kv_lora), BF16),
        jax.ShapeDtypeStruct((n_heads, ms, QK_ROPE), BF16),
    )
    out_specs = (
        pl.BlockSpec((tm, d_pool), prompt_row),
        pl.BlockSpec((tm, kv_lora), prompt_row),
        pl.BlockSpec((tm, QK_ROPE), prompt_row),
        pl.BlockSpec((n_heads, tm, HEAD_W), prompt_head_row),
        pl.BlockSpec((n_heads, tm, HEAD_W), prompt_head_row),
        pl.BlockSpec((tm, n_heads * V_HEAD), prompt_row),
        pl.BlockSpec((ms, d_pool), fixed2),
        pl.BlockSpec((ms, kv_lora), fixed2),
        pl.BlockSpec((ms, QK_ROPE), fixed2),
        pl.BlockSpec((n_heads, ms, kv_lora), fixed3),
        pl.BlockSpec((n_heads, ms, QK_ROPE), fixed3),
    )
    return pl.pallas_call(
        functools.partial(_proj_kernel, n_prompt_tiles=ntp, n_heads=n_heads, d_pool=d_pool,
                          q_lora=q_lora, kv_lora=kv_lora),
        out_shape=out_shape,
        grid=(ntp + 1,),
        in_specs=[
            pl.BlockSpec((tm, d), prompt_row),
            pl.BlockSpec((ms, d), fixed2),
            pl.BlockSpec((tm, 2 * QK_ROPE), lambda i: (jnp.minimum(i, ntp - 1) % tiles_per_seq, 0)),
            whole(cs_s), whole(g_mix), whole(w_in2), whole(g_q), whole(w_q2), whole(g_kv),
            whole(w_kv2), whole(w_uk_t),
        ],
        out_specs=out_specs,
        compiler_params=_params(1),
        name="proj",
    )(hp, hs, cs_p, cs_s, g_mix, w_in2, g_q, w_q2, g_kv, w_kv2, w_uk_t)


def _pool_delta(window_sum, cur, inv_cnt):
    return window_sum * inv_cnt - cur


def _pool_kernel(u_ref, wp_ref, scale_ref, o_ref, ext_ref, *, group):
    i = pl.program_id(1)
    ts = POOL_TILE
    halo = POOL_STATE + 1

    @pl.when(i == 0)
    def _():
        ext_ref[:halo, :] = jnp.zeros((halo, ext_ref.shape[1]), F32)

    @pl.when(i > 0)
    def _():
        ext_ref[:halo, :] = ext_ref[ts:ts + halo, :]

    ext_ref[halo:, :] = u_ref[...]
    pos = i * ts + lax.broadcasted_iota(jnp.int32, (ts, 1), 0)
    for g, w in enumerate(POOL_WINDOWS):
        cols = slice(g * group, (g + 1) * group)
        cur = ext_ref[halo:, cols]
        acc = cur
        for k in range(1, w):
            acc = acc + ext_ref[halo - k:halo - k + ts, cols]
        inv_cnt = 1.0 / jnp.minimum(pos + 1, w).astype(F32)
        delta = _pool_delta(acc, cur, inv_cnt)
        out = _dot(delta.astype(BF16), wp_ref[g])
        o_ref[:, cols] = (out * scale_ref[:, cols]).astype(BF16)


def _pool(u, w_pool, pool_scale, *, batch, seq):
    m, d_pool = u.shape
    ts = POOL_TILE
    nt = seq // ts
    group = d_pool // len(POOL_WINDOWS)
    assert seq % ts == 0 and m == batch * seq
    return pl.pallas_call(
        functools.partial(_pool_kernel, group=group),
        out_shape=jax.ShapeDtypeStruct((m, d_pool), BF16),
        grid=(batch, nt),
        in_specs=[
            pl.BlockSpec((ts, d_pool), lambda b, i: (b * nt + i, 0)),
            pl.BlockSpec(w_pool.shape, lambda b, i: (0, 0, 0)),
            pl.BlockSpec((1, d_pool), lambda b, i: (0, 0)),
        ],
        out_specs=pl.BlockSpec((ts, d_pool), lambda b, i: (b * nt + i, 0)),
        scratch_shapes=[pltpu.VMEM((ts + POOL_STATE + 1, d_pool), F32)],
        compiler_params=_params(2),
        name="pool",
    )(u, w_pool, pool_scale)


def _attn_kernel(q_ref, k_ref, v_ref, o_ref, *, n_heads, sm_scale):
    qi = pl.program_id(1)
    t = ATTN_TILE
    row = lax.broadcasted_iota(jnp.int32, (t, t), 0)
    col = lax.broadcasted_iota(jnp.int32, (t, t), 1)
    causal = row >= col

    for h in range(n_heads):
        q = q_ref[h]

        def block(j, carry, diagonal):
            m, l, acc = carry
            start = pl.multiple_of(j * t, t)
            k = k_ref[h, pl.ds(start, t), :]
            v = v_ref[pl.ds(start, t), h * V_HEAD:(h + 1) * V_HEAD]
            s = _dot_nt(q, k) * sm_scale
            if diagonal:
                s = jnp.where(causal, s, -jnp.inf)
            m_new = jnp.maximum(m, jnp.max(s, axis=1, keepdims=True))
            alpha = jnp.exp(m - m_new)
            p = jnp.exp(s - m_new)
            l = alpha * l + jnp.sum(p, axis=1, keepdims=True)
            acc = alpha * acc + _dot(p.astype(BF16), v)
            return m_new, l, acc

        init = (jnp.full((t, 1), -jnp.inf, F32), jnp.zeros((t, 1), F32), jnp.zeros((t, V_HEAD), F32))
        carry = lax.fori_loop(0, qi, lambda j, c: block(j, c, False), init)
        _, l, acc = block(qi, carry, True)
        o_ref[:, h * V_HEAD:(h + 1) * V_HEAD] = (acc / l).astype(BF16)


def _attn(q, k, v, *, batch, seq, sm_scale):
    n_heads, m, _ = q.shape
    t = ATTN_TILE
    nq = seq // t
    assert seq % t == 0
    return pl.pallas_call(
        functools.partial(_attn_kernel, n_heads=n_heads, sm_scale=sm_scale),
        out_shape=jax.ShapeDtypeStruct((m, n_heads * V_HEAD), BF16),
        grid=(batch, nq),
        in_specs=[
            pl.BlockSpec((n_heads, t, HEAD_W), lambda b, i: (0, b * nq + i, 0)),
            pl.BlockSpec((n_heads, seq, HEAD_W), lambda b, i: (0, b, 0)),
            pl.BlockSpec((seq, n_heads * V_HEAD), lambda b, i: (b, 0)),
        ],
        out_specs=pl.BlockSpec((t, n_heads * V_HEAD), lambda b, i: (b * nq + i, 0)),
        compiler_params=_params(2),
        name="attn",
    )(q, k, v)


def _decode_kernel(pt_ref, qlat_ref, qpe_ref, ckvs_ref, kpes_ref, kv_hbm, rope_hbm, o_ref,
                   kv_buf, rope_buf, kv_sem, rope_sem, *, n_pages, sm_scale):
    b = pl.program_id(0)
    n_samples = pl.num_programs(0)
    slot = b % 2

    def page_copies(sample, sl):
        copies = []
        for i in range(n_pages):
            page_id = pt_ref[sample * n_pages + i]
            copies.append(pltpu.make_async_copy(kv_hbm.at[page_id], kv_buf.at[sl, i], kv_sem.at[sl]))
            copies.append(pltpu.make_async_copy(rope_hbm.at[page_id], rope_buf.at[sl, i], rope_sem.at[sl]))
        return copies

    @pl.when(b == 0)
    def _():
        for c in page_copies(0, 0):
            c.start()

    @pl.when(b + 1 < n_samples)
    def _():
        for c in page_copies(b + 1, 1 - slot):
            c.start()

    for c in page_copies(b, slot):
        c.wait()

    qlat = qlat_ref[b]
    qpe = qpe_ref[b]
    own_ckv = ckvs_ref[pl.ds(b, 1), :].astype(BF16).astype(F32)
    own_kpe = kpes_ref[pl.ds(b, 1), :].astype(BF16).astype(F32)
    s_own = (jnp.sum(qlat.astype(F32) * own_ckv, axis=1, keepdims=True)
             + jnp.sum(qpe.astype(F32) * own_kpe, axis=1, keepdims=True)) * sm_scale

    chunk = DECODE_CHUNK_PAGES
    ckv_parts, s_parts = [], []
    for c0 in range(0, n_pages, chunk):
        ckv = jnp.concatenate([kv_buf[slot, i] for i in range(c0, c0 + chunk)], axis=0).astype(BF16)
        kpe_t = jnp.concatenate([rope_buf[slot, i] for i in range(c0, c0 + chunk)], axis=1).astype(BF16)
        ckv_parts.append(ckv)
        s_parts.append((_dot_nt(qlat, ckv) + _dot(qpe, kpe_t)) * sm_scale)
    m = s_own
    for s in s_parts:
        m = jnp.maximum(m, jnp.max(s, axis=1, keepdims=True))
    p_own = jnp.exp(s_own - m)
    l = p_own
    acc = p_own.astype(BF16).astype(F32) * own_ckv
    for s, ckv in zip(s_parts, ckv_parts):
        p = jnp.exp(s - m)
        l = l + jnp.sum(p, axis=1, keepdims=True)
        acc = acc + _dot(p.astype(BF16), ckv)
    o_ref[b] = acc / l


def _decode(page_table, q_lat, q_pe, ckv_own, kpe_own, cache_kv, cache_rope_t, *, sm_scale):
    bs, n_heads, kv_lora = q_lat.shape
    n_pages = page_table.shape[1]
    page = cache_kv.shape[1]
    assert n_pages % DECODE_CHUNK_PAGES == 0
    whole = lambda a: pl.BlockSpec(a.shape, lambda b, pt, nd=a.ndim: (0,) * nd)
    grid_spec = pltpu.PrefetchScalarGridSpec(
        num_scalar_prefetch=1,
        grid=(bs,),
        in_specs=[whole(q_lat), whole(q_pe), whole(ckv_own), whole(kpe_own),
                  pl.BlockSpec(memory_space=pl.ANY), pl.BlockSpec(memory_space=pl.ANY)],
        out_specs=pl.BlockSpec((bs, n_heads, kv_lora), lambda b, pt: (0, 0, 0)),
        scratch_shapes=[
            pltpu.VMEM((2, n_pages, page, kv_lora), F32),
            pltpu.VMEM((2, n_pages, QK_ROPE, page), F32),
            pltpu.SemaphoreType.DMA((2,)),
            pltpu.SemaphoreType.DMA((2,)),
        ],
    )
    return pl.pallas_call(
        functools.partial(_decode_kernel, n_pages=n_pages, sm_scale=sm_scale),
        out_shape=jax.ShapeDtypeStruct((bs, n_heads, kv_lora), F32),
        grid_spec=grid_spec,
        compiler_params=_params(1),
        name="decode",
    )(page_table.reshape(-1), q_lat, q_pe, ckv_own, kpe_own, cache_kv, cache_rope_t)


def _sample_kernel(olat_ref, wuv_ref, sp_ref, us_ref, wp_ref, scale_ref, pool_ref, attn_ref,
                   *, n_heads, group, past_len):
    for h in range(n_heads):
        attn_ref[:, h * V_HEAD:(h + 1) * V_HEAD] = _dot(olat_ref[h].astype(BF16), wuv_ref[h]).astype(BF16)
    for g, w in enumerate(POOL_WINDOWS):
        cols = slice(g * group, (g + 1) * group)
        cur = us_ref[:, cols]
        hist = sp_ref[:, POOL_STATE - (w - 1):, cols]
        acc = cur + jnp.sum(hist, axis=1)
        delta = _pool_delta(acc, cur, 1.0 / min(past_len + 1, w))
        out = _dot(delta.astype(BF16), wp_ref[g])
        pool_ref[:, cols] = (out * scale_ref[:, cols]).astype(BF16)


def _sample_mix(o_lat, w_uv_h, state_pool, u_s, w_pool, pool_scale, *, past_len):
    n_heads, ms, _ = o_lat.shape
    d_pool = u_s.shape[1]
    return pl.pallas_call(
        functools.partial(_sample_kernel, n_heads=n_heads, group=d_pool // len(POOL_WINDOWS),
                          past_len=past_len),
        out_shape=(jax.ShapeDtypeStruct((ms, d_pool), BF16),
                   jax.ShapeDtypeStruct((ms, n_heads * V_HEAD), BF16)),
        compiler_params=pltpu.CompilerParams(vmem_limit_bytes=VMEM_LIMIT),
        name="sample_mix",
    )(o_lat, w_uv_h, state_pool, u_s, w_pool, pool_scale)


def _outproj_kernel(hp_ref, poolp_ref, attnp_ref, hs_ref, pools_ref, attns_ref, wo_ref, g_ref,
                    yp_ref, ys_ref, *, n_prompt_tiles, d_pool):
    i = pl.program_id(0)

    def body(h_ref, pool_ref, attn_ref, y_ref):
        mix = _dot(pool_ref[...], wo_ref[:d_pool, :]) + _dot(attn_ref[...], wo_ref[d_pool:, :])
        y_ref[...] = h_ref[...] + _rms(mix, g_ref[...])

    @pl.when(i < n_prompt_tiles)
    def _():
        body(hp_ref, poolp_ref, attnp_ref, yp_ref)

    @pl.when(i == n_prompt_tiles)
    def _():
        body(hs_ref, pools_ref, attns_ref, ys_ref)


def _outproj(hp, pool_p, attn_p, hs, pool_s, attn_s, w_out, g_post):
    mp, d = hp.shape
    ms = hs.shape[0]
    d_pool = pool_p.shape[1]
    d_attn = attn_p.shape[1]
    tm = ROW_TILE
    ntp = mp // tm
    prompt_row = lambda i: (jnp.minimum(i, ntp - 1), 0)
    fixed = lambda i: (0, 0)
    return pl.pallas_call(
        functools.partial(_outproj_kernel, n_prompt_tiles=ntp, d_pool=d_pool),
        out_shape=(jax.ShapeDtypeStruct((mp, d), F32), jax.ShapeDtypeStruct((ms, d), F32)),
        grid=(ntp + 1,),
        in_specs=[
            pl.BlockSpec((tm, d), prompt_row),
            pl.BlockSpec((tm, d_pool), prompt_row),
            pl.BlockSpec((tm, d_attn), prompt_row),
            pl.BlockSpec((ms, d), fixed),
            pl.BlockSpec((ms, d_pool), fixed),
            pl.BlockSpec((ms, d_attn), fixed),
            pl.BlockSpec(w_out.shape, fixed),
            pl.BlockSpec((1, d), fixed),
        ],
        out_specs=(pl.BlockSpec((tm, d), prompt_row), pl.BlockSpec((ms, d), fixed)),
        compiler_params=_params(1),
        name="outproj",
    )(hp, pool_p, attn_p, hs, pool_s, attn_s, w_out, g_post)


def _rotate_half_columns(w):
    half = QK_ROPE // 2
    return jnp.concatenate([-w[..., half:], w[..., :half]], axis=-1)


def _rope_table(pos):
    half = QK_ROPE // 2
    inv = ROPE_THETA ** (-jnp.arange(half, dtype=F32) / half)
    ang = pos.astype(F32)[:, None] * inv[None, :]
    cos, sin = jnp.cos(ang), jnp.sin(ang)
    return jnp.concatenate([cos, cos, sin, sin], axis=-1)


def kernel(x_prompt, x_sample, cache_kv_latent, cache_k_rope, state_pool, page_table,
           g_ffn1_pre, w1_gate, w1_up, w1_down, g_ffn1_post,
           g_mix_pre, w_in, w_pool, pool_scale, g_q, w_uq, g_kv, w_uk, w_uv, w_out, g_mix_post,
           g_ffn2_pre, w2_gate, w2_up, w2_down, g_ffn2_post):
    depth = w_in.shape[0]
    bp, sp, d = x_prompt.shape
    bs, ts, _ = x_sample.shape
    n_pages = page_table.shape[1]
    page = cache_kv_latent.shape[2]
    past_len = n_pages * page
    kv_lora, n_heads, _ = w_uk.shape[1:]
    q_lora = w_uq.shape[1]
    d_pool = w_pool.shape[1] * w_pool.shape[2]
    assert ts == 1, "the decode kernel handles one new token per sample sequence"
    sm_scale = (QK_NOPE + QK_ROPE) ** -0.5

    cs_p = _rope_table(jnp.arange(sp, dtype=jnp.int32))
    cs_s = jnp.tile(_rope_table(past_len + jnp.arange(ts, dtype=jnp.int32)), (bs, 1))

    hp = x_prompt.reshape(bp * sp, d)
    hs = x_sample.reshape(bs * ts, d)
    outs = [[] for _ in range(6)]
    for l in range(depth):
        row = lambda g: g[l][None, :]
        w_in_l = w_in[l]
        w_in2 = jnp.concatenate(
            [w_in_l, _rotate_half_columns(w_in_l[:, d_pool + q_lora + kv_lora:])], axis=1).astype(BF16)
        wq = w_uq[l].reshape(q_lora, n_heads, QK_NOPE + QK_ROPE)
        w_q2 = jnp.concatenate(
            [wq, _rotate_half_columns(wq[..., QK_NOPE:])], axis=-1).reshape(q_lora, n_heads * HEAD_W).astype(BF16)
        w_kv2 = jnp.concatenate(
            [w_uk[l].reshape(kv_lora, -1), w_uv[l].reshape(kv_lora, -1)], axis=1).astype(BF16)
        w_uk_t = jnp.transpose(w_uk[l], (1, 2, 0)).astype(BF16)
        w_uv_h = jnp.transpose(w_uv[l], (1, 0, 2)).astype(BF16)
        w_pool_l = w_pool[l].astype(BF16)
        w_out_l = w_out[l].astype(BF16)

        hp, hs = _ffn(hp, hs, row(g_ffn1_pre), w1_gate[l].astype(BF16), w1_up[l].astype(BF16),
                      w1_down[l].astype(BF16), row(g_ffn1_post))
        (u_p, ckv_p, kpe_p, q_p, k_p, v_p, u_s, ckv_s, kpe_s, qlat_s, qpe_s) = _proj(
            hp, hs, cs_p, cs_s, row(g_mix_pre), w_in2, row(g_q), w_q2, row(g_kv), w_kv2, w_uk_t,
            seq=sp, n_heads=n_heads)
        pool_p = _pool(u_p, w_pool_l, row(pool_scale), batch=bp, seq=sp)
        attn_p = _attn(q_p, k_p, v_p, batch=bp, seq=sp, sm_scale=sm_scale)
        o_lat = _decode(page_table, jnp.transpose(qlat_s, (1, 0, 2)), jnp.transpose(qpe_s, (1, 0, 2)),
                        ckv_s, kpe_s, cache_kv_latent[l],
                        jnp.swapaxes(cache_k_rope[l], 1, 2),
                        sm_scale=sm_scale)
        pool_s, attn_s = _sample_mix(jnp.transpose(o_lat, (1, 0, 2)), w_uv_h, state_pool[l], u_s,
                                     w_pool_l, row(pool_scale), past_len=past_len)
        hp, hs = _outproj(hp, pool_p, attn_p, hs, pool_s, attn_s, w_out_l, row(g_mix_post))
        hp, hs = _ffn(hp, hs, row(g_ffn2_pre), w2_gate[l].astype(BF16), w2_up[l].astype(BF16),
                      w2_down[l].astype(BF16), row(g_ffn2_post))

        u_p3 = u_p.reshape(bp, sp, d_pool)
        outs[0].append(ckv_p.reshape(bp, sp, kv_lora))
        outs[1].append(kpe_p.reshape(bp, sp, QK_ROPE))
        outs[2].append(u_p3[:, sp - POOL_STATE:])
        outs[3].append(ckv_s.reshape(bs, ts, kv_lora))
        outs[4].append(kpe_s.reshape(bs, ts, QK_ROPE))
        outs[5].append(jnp.concatenate([state_pool[l][:, ts:], u_s.reshape(bs, ts, d_pool)], axis=1))
    return (hp.reshape(bp, sp, d), hs.reshape(bs, ts, d), *[jnp.stack(o) for o in outs])
```

```python
import functools
import math

import jax
import jax.numpy as jnp
from jax import lax
from jax.experimental import pallas as pl
from jax.experimental.pallas import tpu as pltpu

F32 = jnp.float32
BF16 = jnp.bfloat16

EPS = 1e-6
ROPE_THETA = 10000.0
POOL_WINDOWS = (2, 4, 8, 16)
POOL_STATE = max(POOL_WINDOWS) - 1
QK_NOPE = 128
QK_ROPE = 64
V_HEAD = 128
HEAD_W = 2 * 128

V7X_VMEM_BYTES = 64 * 1024 * 1024
VMEM_LIMIT = V7X_VMEM_BYTES - 8 * 1024 * 1024

ROW_TILE = 512
FFN_ROW_TILE = 1024
FF_TILE = 256
NORM_ROWS = 128
PROJ_TILE = 256
POOL_TILE = 512
ATTN_TILE = 512
ATTN_HEADS_IN_FLIGHT = 4
DECODE_CHUNK_PAGES = 32


def _params(n_grid_dims):
    return pltpu.CompilerParams(
        dimension_semantics=("arbitrary",) * n_grid_dims, vmem_limit_bytes=VMEM_LIMIT)


def _rms(x, g):
    return x * lax.rsqrt(jnp.mean(x * x, axis=-1, keepdims=True) + EPS) * g


def _dot(a, b):
    return jnp.dot(a, b, preferred_element_type=F32)


def _dot_nt(a, b):
    return lax.dot_general(a, b, (((1,), (1,)), ((), ())), preferred_element_type=F32)


def _for_row_chunks(rows, fn):
    n_chunks, rem = divmod(rows, NORM_ROWS)
    assert rem == 0

    def step(c, carry):
        fn(pl.ds(pl.multiple_of(c * NORM_ROWS, NORM_ROWS), NORM_ROWS))
        return carry

    lax.fori_loop(0, n_chunks, step, 0, unroll=2 if n_chunks % 2 == 0 else 1)


def _ffn_kernel(xp_ref, xs_ref, gpre_ref, wg_ref, wu_ref, wd_ref, gpost_ref,
                yp_ref, ys_ref, xnp_ref, xns_ref, *, n_ff_tiles):
    last_tile = pl.program_id(0) == pl.num_programs(0) - 1
    first_ff = pl.program_id(1) == 0
    last_ff = pl.program_id(1) == n_ff_tiles - 1
    tm = xp_ref.shape[0]

    def prologue(x_ref, y_ref, xn_ref):
        def chunk(rs):
            xn_ref[rs, :] = _rms(x_ref[rs, :], gpre_ref[...]).astype(BF16)
            y_ref[rs, :] = jnp.zeros((NORM_ROWS, y_ref.shape[1]), F32)

        _for_row_chunks(x_ref.shape[0], chunk)

    def epilogue(x_ref, y_ref):
        def chunk(rs):
            y_ref[rs, :] = x_ref[rs, :] + 0.5 * _rms(y_ref[rs, :], gpost_ref[...])

        _for_row_chunks(x_ref.shape[0], chunk)

    def swiglu_part(xn):
        gate = _dot(xn, wg_ref[...].astype(BF16))
        up = _dot(xn, wu_ref[...].astype(BF16))
        hidden = (gate / (1.0 + jnp.exp(-gate))) * up
        return _dot(hidden.astype(BF16), wd_ref[...].astype(BF16))

    pl.when(first_ff)(lambda: prologue(xp_ref, yp_ref, xnp_ref))
    pl.when(first_ff & last_tile)(lambda: prologue(xs_ref, ys_ref, xns_ref))

    @pl.when(jnp.logical_not(last_tile))
    def _():
        yp_ref[...] += swiglu_part(xnp_ref[...])

    @pl.when(last_tile)
    def _():
        part = swiglu_part(jnp.concatenate([xnp_ref[...], xns_ref[...]], axis=0))
        yp_ref[...] += part[:tm]
        ys_ref[...] += part[tm:]

    pl.when(last_ff)(lambda: epilogue(xp_ref, yp_ref))
    pl.when(last_ff & last_tile)(lambda: epilogue(xs_ref, ys_ref))


def _ffn(xp, xs, g_pre, w_gate, w_up, w_down, g_post):
    mp, d = xp.shape
    ms = xs.shape[0]
    d_ff = w_gate.shape[1]
    tm = FFN_ROW_TILE
    nj = d_ff // FF_TILE
    assert mp % tm == 0 and d_ff % FF_TILE == 0
    prompt_row = lambda i, j: (i, 0)
    fixed = lambda i, j: (0, 0)
    once = pl.Buffered(1)
    return pl.pallas_call(
        functools.partial(_ffn_kernel, n_ff_tiles=nj),
        out_shape=(jax.ShapeDtypeStruct((mp, d), F32), jax.ShapeDtypeStruct((ms, d), F32)),
        grid=(mp // tm, nj),
        in_specs=[
            pl.BlockSpec((tm, d), prompt_row, pipeline_mode=once),
            pl.BlockSpec((ms, d), fixed, pipeline_mode=once),
            pl.BlockSpec((1, d), fixed, pipeline_mode=once),
            pl.BlockSpec((d, FF_TILE), lambda i, j: (0, j)),
            pl.BlockSpec((d, FF_TILE), lambda i, j: (0, j)),
            pl.BlockSpec((FF_TILE, d), lambda i, j: (j, 0)),
            pl.BlockSpec((1, d), fixed, pipeline_mode=once),
        ],
        out_specs=(pl.BlockSpec((tm, d), prompt_row), pl.BlockSpec((ms, d), fixed)),
        scratch_shapes=[pltpu.VMEM((tm, d), BF16), pltpu.VMEM((ms, d), BF16)],
        compiler_params=_params(2),
        name="ffn",
    )(xp, xs, g_pre, w_gate, w_up, w_down, g_post)


def _rope_pairs(slab, cs):
    t = slab * cs
    return t + pltpu.roll(t, QK_ROPE, axis=1)


def _proj_kernel(hp_ref, hs_ref, csp_ref, css_ref, gmix_ref, win_ref, gq_ref, wq_ref, gkv_ref,
                 wkv_ref, wukt_ref,
                 up_ref, ckvp_ref, kpep_ref, q_ref, k_ref, v_ref,
                 us_ref, ckvs_ref, kpes_ref, qlat_ref, qpes_ref,
                 *, n_prompt_tiles, n_heads, d_pool, q_lora, kv_lora):
    i = pl.program_id(0)

    def common(h_ref, cs_ref, u_ref, ckv_ref, kpe_ref):
        hn = _rms(h_ref[...], gmix_ref[...]).astype(BF16)
        z = _dot(hn, win_ref[...])
        u_ref[...] = z[:, :d_pool]
        cqn = _rms(z[:, d_pool:d_pool + q_lora], gq_ref[...]).astype(BF16)
        ckv = _rms(z[:, d_pool + q_lora:d_pool + q_lora + kv_lora], gkv_ref[...])
        ckv_ref[...] = ckv
        cs = cs_ref[...]
        kpe = _rope_pairs(z[:, d_pool + q_lora + kv_lora:], cs)
        kpe_ref[...] = kpe[:, :QK_ROPE]
        qq = _dot(cqn, wq_ref[...])
        return ckv, kpe, qq, cs

    @pl.when(i < n_prompt_tiles)
    def _():
        ckv, kpe, qq, cs = common(hp_ref, csp_ref, up_ref, ckvp_ref, kpep_ref)
        lane = lax.broadcasted_iota(jnp.int32, kpe.shape, 1)
        kpe_pad = jnp.where(lane < QK_ROPE, kpe, 0.0).astype(BF16)
        kv = _dot(ckv.astype(BF16), wkv_ref[...])
        for h in range(n_heads):
            v_lo = n_heads * QK_NOPE + h * V_HEAD
            v_ref[h] = kv[:, v_lo:v_lo + V_HEAD].astype(BF16)
            slab = qq[:, h * HEAD_W:(h + 1) * HEAD_W]
            q_ref[h, :, :QK_NOPE] = slab[:, :QK_NOPE].astype(BF16)
            q_ref[h, :, QK_NOPE:] = _rope_pairs(slab[:, QK_NOPE:], cs).astype(BF16)
            k_ref[h, :, :QK_NOPE] = kv[:, h * QK_NOPE:(h + 1) * QK_NOPE].astype(BF16)
            k_ref[h, :, QK_NOPE:] = kpe_pad

    @pl.when(i == n_prompt_tiles)
    def _():
        _, _, qq, cs = common(hs_ref, css_ref, us_ref, ckvs_ref, kpes_ref)
        for h in range(n_heads):
            slab = qq[:, h * HEAD_W:(h + 1) * HEAD_W]
            qlat_ref[h] = _dot(slab[:, :QK_NOPE].astype(BF16), wukt_ref[h]).astype(BF16)
            qpes_ref[h] = _rope_pairs(slab[:, QK_NOPE:], cs)[:, :QK_ROPE].astype(BF16)


def _proj(hp, hs, cs_p, cs_s, g_mix, w_in2, g_q, w_q2, g_kv, w_kv2, w_uk_t, *, seq, n_heads):
    mp, d = hp.shape
    ms = hs.shape[0]
    tm = PROJ_TILE
    ntp = mp // tm
    tiles_per_seq = seq // tm
    assert mp % tm == 0 and seq % tm == 0
    d_in2 = w_in2.shape[1]
    q_lora = w_q2.shape[0]
    kv_lora = w_kv2.shape[0]
    d_pool = d_in2 - q_lora - kv_lora - 2 * QK_ROPE
    prompt_row = lambda i: (jnp.minimum(i, ntp - 1), 0)
    prompt_head_row = lambda i: (0, jnp.minimum(i, ntp - 1), 0)
    fixed2 = lambda i: (0, 0)
    fixed3 = lambda i: (0, 0, 0)
    whole = lambda a: pl.BlockSpec(a.shape, fixed2 if a.ndim == 2 else fixed3)
    out_shape = (
        jax.ShapeDtypeStruct((mp, d_pool), F32),
        jax.ShapeDtypeStruct((mp, kv_lora), F32),
        jax.ShapeDtypeStruct((mp, QK_ROPE), F32),
        jax.ShapeDtypeStruct((n_heads, mp, HEAD_W), BF16),
        jax.ShapeDtypeStruct((n_heads, mp, HEAD_W), BF16),
        jax.ShapeDtypeStruct((n_heads, mp, V_HEAD), BF16),
        jax.ShapeDtypeStruct((ms, d_pool), F32),
        jax.ShapeDtypeStruct((ms, kv_lora), F32),
        jax.ShapeDtypeStruct((ms, QK_ROPE), F32),
        jax.ShapeDtypeStruct((n_heads, ms, kv_lora), BF16),
        jax.ShapeDtypeStruct((n_heads, ms, QK_ROPE), BF16),
    )
    out_specs = (
        pl.BlockSpec((tm, d_pool), prompt_row),
        pl.BlockSpec((tm, kv_lora), prompt_row),
        pl.BlockSpec((tm, QK_ROPE), prompt_row),
        pl.BlockSpec((n_heads, tm, HEAD_W), prompt_head_row),
        pl.BlockSpec((n_heads, tm, HEAD_W), prompt_head_row),
        pl.BlockSpec((n_heads, tm, V_HEAD), prompt_head_row),
        pl.BlockSpec((ms, d_pool), fixed2),
        pl.BlockSpec((ms, kv_lora), fixed2),
        pl.BlockSpec((ms, QK_ROPE), fixed2),
        pl.BlockSpec((n_heads, ms, kv_lora), fixed3),
        pl.BlockSpec((n_heads, ms, QK_ROPE), fixed3),
    )
    return pl.pallas_call(
        functools.partial(_proj_kernel, n_prompt_tiles=ntp, n_heads=n_heads, d_pool=d_pool,
                          q_lora=q_lora, kv_lora=kv_lora),
        out_shape=out_shape,
        grid=(ntp + 1,),
        in_specs=[
            pl.BlockSpec((tm, d), prompt_row),
            pl.BlockSpec((ms, d), fixed2),
            pl.BlockSpec((tm, 2 * QK_ROPE), lambda i: (jnp.minimum(i, ntp - 1) % tiles_per_seq, 0)),
            whole(cs_s), whole(g_mix), whole(w_in2), whole(g_q), whole(w_q2), whole(g_kv),
            whole(w_kv2), whole(w_uk_t),
        ],
        out_specs=out_specs,
        compiler_params=_params(1),
        name="proj",
    )(hp, hs, cs_p, cs_s, g_mix, w_in2, g_q, w_q2, g_kv, w_kv2, w_uk_t)


def _pool_delta(window_sum, cur, inv_cnt):
    return window_sum * inv_cnt - cur


def _pool_kernel(u_ref, wp_ref, scale_ref, o_ref, ext_ref, *, group):
    i = pl.program_id(1)
    ts = POOL_TILE
    halo = POOL_STATE + 1

    @pl.when(i == 0)
    def _():
        ext_ref[:halo, :] = jnp.zeros((halo, ext_ref.shape[1]), F32)

    @pl.when(i > 0)
    def _():
        ext_ref[:halo, :] = ext_ref[ts:ts + halo, :]

    ext_ref[halo:, :] = u_ref[...]
    pos = i * ts + lax.broadcasted_iota(jnp.int32, (ts, 1), 0)
    for g, w in enumerate(POOL_WINDOWS):
        cols = slice(g * group, (g + 1) * group)
        cur = ext_ref[halo:, cols]
        acc = cur
        for k in range(1, w):
            acc = acc + ext_ref[halo - k:halo - k + ts, cols]
        inv_cnt = 1.0 / jnp.minimum(pos + 1, w).astype(F32)
        delta = _pool_delta(acc, cur, inv_cnt)
        out = _dot(delta.astype(BF16), wp_ref[g])
        o_ref[:, cols] = (out * scale_ref[:, cols]).astype(BF16)


def _pool(u, w_pool, pool_scale, *, batch, seq):
    m, d_pool = u.shape
    ts = POOL_TILE
    nt = seq // ts
    group = d_pool // len(POOL_WINDOWS)
    assert seq % ts == 0 and m == batch * seq
    return pl.pallas_call(
        functools.partial(_pool_kernel, group=group),
        out_shape=jax.ShapeDtypeStruct((m, d_pool), BF16),
        grid=(batch, nt),
        in_specs=[
            pl.BlockSpec((ts, d_pool), lambda b, i: (b * nt + i, 0)),
            pl.BlockSpec(w_pool.shape, lambda b, i: (0, 0, 0)),
            pl.BlockSpec((1, d_pool), lambda b, i: (0, 0)),
        ],
        out_specs=pl.BlockSpec((ts, d_pool), lambda b, i: (b * nt + i, 0)),
        scratch_shapes=[pltpu.VMEM((ts + POOL_STATE + 1, d_pool), F32)],
        compiler_params=_params(2),
        name="pool",
    )(u, w_pool, pool_scale)


def _attn_kernel(q_ref, k_ref, v_ref, o_ref, *, n_heads, n_q_tiles, exp2_scale):
    qi = pl.program_id(1)
    t = ATTN_TILE
    row = lax.broadcasted_iota(jnp.int32, (t, t), 0)
    col = lax.broadcasted_iota(jnp.int32, (t, t), 1)
    causal = row >= col

    def tile(n_past):
        def head(h, carry):
            q = q_ref[h]
            s_diag = jnp.where(causal, _dot_nt(q, k_ref[h, n_past:n_past + t, :]), -jnp.inf)
            m = jnp.max(s_diag, axis=1, keepdims=True)
            if n_past:
                s_past = _dot_nt(q, k_ref[h, :n_past, :])
                m = jnp.maximum(m, jnp.max(s_past, axis=1, keepdims=True))
            p_diag = jnp.exp2((s_diag - m) * exp2_scale)
            l = jnp.sum(p_diag, axis=1, keepdims=True)
            o = _dot(p_diag.astype(BF16), v_ref[h, n_past:n_past + t, :])
            if n_past:
                p_past = jnp.exp2((s_past - m) * exp2_scale)
                l = l + jnp.sum(p_past, axis=1, keepdims=True)
                o = o + _dot(p_past.astype(BF16), v_ref[h, :n_past, :])
            o_ref[h] = (o / l).astype(BF16)
            return carry

        lax.fori_loop(0, n_heads, head, 0, unroll=ATTN_HEADS_IN_FLIGHT)

    for kq in range(n_q_tiles):
        pl.when(qi == kq)(functools.partial(tile, kq * t))


def _attn(q, k, v, *, batch, seq, sm_scale):
    n_heads, m, _ = q.shape
    t = ATTN_TILE
    nq = seq // t
    assert seq % t == 0
    return pl.pallas_call(
        functools.partial(_attn_kernel, n_heads=n_heads, n_q_tiles=nq,
                          exp2_scale=sm_scale * math.log2(math.e)),
        out_shape=jax.ShapeDtypeStruct((n_heads, m, V_HEAD), BF16),
        grid=(batch, nq),
        in_specs=[
            pl.BlockSpec((n_heads, t, HEAD_W), lambda b, i: (0, b * nq + i, 0)),
            pl.BlockSpec((n_heads, seq, HEAD_W), lambda b, i: (0, b, 0)),
            pl.BlockSpec((n_heads, seq, V_HEAD), lambda b, i: (0, b, 0)),
        ],
        out_specs=pl.BlockSpec((n_heads, t, V_HEAD), lambda b, i: (0, b * nq + i, 0)),
        compiler_params=_params(2),
        name="attn",
    )(q, k, v)


def _decode_kernel(pt_ref, qlat_ref, qpe_ref, ckvs_ref, kpes_ref, kv_hbm, rope_hbm, o_ref,
                   kv_buf, rope_buf, kv_sem, rope_sem, *, n_pages, sm_scale):
    b = pl.program_id(0)
    n_samples = pl.num_programs(0)
    slot = b % 2

    def page_copies(sample, sl):
        copies = []
        for i in range(n_pages):
            page_id = pt_ref[sample * n_pages + i]
            copies.append(pltpu.make_async_copy(kv_hbm.at[page_id], kv_buf.at[sl, i], kv_sem.at[sl]))
            copies.append(pltpu.make_async_copy(rope_hbm.at[page_id], rope_buf.at[sl, i], rope_sem.at[sl]))
        return copies

    @pl.when(b == 0)
    def _():
        for c in page_copies(0, 0):
            c.start()

    @pl.when(b + 1 < n_samples)
    def _():
        for c in page_copies(b + 1, 1 - slot):
            c.start()

    for c in page_copies(b, slot):
        c.wait()

    qlat = qlat_ref[b]
    qpe = qpe_ref[b]
    own_ckv = ckvs_ref[pl.ds(b, 1), :].astype(BF16).astype(F32)
    own_kpe = kpes_ref[pl.ds(b, 1), :].astype(BF16).astype(F32)
    s_own = (jnp.sum(qlat.astype(F32) * own_ckv, axis=1, keepdims=True)
             + jnp.sum(qpe.astype(F32) * own_kpe, axis=1, keepdims=True)) * sm_scale

    chunk = DECODE_CHUNK_PAGES
    ckv_parts, s_parts = [], []
    for c0 in range(0, n_pages, chunk):
        ckv = jnp.concatenate([kv_buf[slot, i] for i in range(c0, c0 + chunk)], axis=0).astype(BF16)
        kpe_t = jnp.concatenate([rope_buf[slot, i] for i in range(c0, c0 + chunk)], axis=1).astype(BF16)
        ckv_parts.append(ckv)
        s_parts.append((_dot_nt(qlat, ckv) + _dot(qpe, kpe_t)) * sm_scale)
    m = s_own
    for s in s_parts:
        m = jnp.maximum(m, jnp.max(s, axis=1, keepdims=True))
    p_own = jnp.exp(s_own - m)
    l = p_own
    acc = p_own.astype(BF16).astype(F32) * own_ckv
    for s, ckv in zip(s_parts, ckv_parts):
        p = jnp.exp(s - m)
        l = l + jnp.sum(p, axis=1, keepdims=True)
        acc = acc + _dot(p.astype(BF16), ckv)
    o_ref[b] = acc / l


def _decode(page_table, q_lat, q_pe, ckv_own, kpe_own, cache_kv, cache_rope_t, *, sm_scale):
    bs, n_heads, kv_lora = q_lat.shape
    n_pages = page_table.shape[1]
    page = cache_kv.shape[1]
    assert n_pages % DECODE_CHUNK_PAGES == 0
    whole = lambda a: pl.BlockSpec(a.shape, lambda b, pt, nd=a.ndim: (0,) * nd)
    grid_spec = pltpu.PrefetchScalarGridSpec(
        num_scalar_prefetch=1,
        grid=(bs,),
        in_specs=[whole(q_lat), whole(q_pe), whole(ckv_own), whole(kpe_own),
                  pl.BlockSpec(memory_space=pl.ANY), pl.BlockSpec(memory_space=pl.ANY)],
        out_specs=pl.BlockSpec((bs, n_heads, kv_lora), lambda b, pt: (0, 0, 0)),
        scratch_shapes=[
            pltpu.VMEM((2, n_pages, page, kv_lora), F32),
            pltpu.VMEM((2, n_pages, QK_ROPE, page), F32),
            pltpu.SemaphoreType.DMA((2,)),
            pltpu.SemaphoreType.DMA((2,)),
        ],
    )
    return pl.pallas_call(
        functools.partial(_decode_kernel, n_pages=n_pages, sm_scale=sm_scale),
        out_shape=jax.ShapeDtypeStruct((bs, n_heads, kv_lora), F32),
        grid_spec=grid_spec,
        compiler_params=_params(1),
        name="decode",
    )(page_table.reshape(-1), q_lat, q_pe, ckv_own, kpe_own, cache_kv, cache_rope_t)


def _sample_kernel(olat_ref, wuv_ref, sp_ref, us_ref, wp_ref, scale_ref, pool_ref, attn_ref,
                   *, n_heads, group, past_len):
    for h in range(n_heads):
        attn_ref[:, h * V_HEAD:(h + 1) * V_HEAD] = _dot(olat_ref[h].astype(BF16), wuv_ref[h]).astype(BF16)
    for g, w in enumerate(POOL_WINDOWS):
        cols = slice(g * group, (g + 1) * group)
        cur = us_ref[:, cols]
        acc = cur
        for k in range(1, w):
            acc = acc + sp_ref[POOL_STATE - k, :, cols]
        delta = _pool_delta(acc, cur, 1.0 / min(past_len + 1, w))
        out = _dot(delta.astype(BF16), wp_ref[g])
        pool_ref[:, cols] = (out * scale_ref[:, cols]).astype(BF16)


def _sample_mix(o_lat, w_uv_h, state_pool, u_s, w_pool, pool_scale, *, past_len):
    n_heads, ms, _ = o_lat.shape
    d_pool = u_s.shape[1]
    return pl.pallas_call(
        functools.partial(_sample_kernel, n_heads=n_heads, group=d_pool // len(POOL_WINDOWS),
                          past_len=past_len),
        out_shape=(jax.ShapeDtypeStruct((ms, d_pool), BF16),
                   jax.ShapeDtypeStruct((ms, n_heads * V_HEAD), BF16)),
        compiler_params=pltpu.CompilerParams(vmem_limit_bytes=VMEM_LIMIT),
        name="sample_mix",
    )(o_lat, w_uv_h, state_pool, u_s, w_pool, pool_scale)


def _outproj_kernel(hp_ref, poolp_ref, attnp_ref, hs_ref, pools_ref, attns_ref, wo_ref, g_ref,
                    yp_ref, ys_ref, *, n_prompt_tiles, d_pool):
    i = pl.program_id(0)

    def body(h_ref, pool_ref, attn, y_ref):
        mix = _dot(pool_ref[...], wo_ref[:d_pool, :]) + _dot(attn, wo_ref[d_pool:, :])
        y_ref[...] = h_ref[...] + _rms(mix, g_ref[...])

    @pl.when(i < n_prompt_tiles)
    def _():
        heads = [attnp_ref[h] for h in range(attnp_ref.shape[0])]
        body(hp_ref, poolp_ref, jnp.concatenate(heads, axis=1), yp_ref)

    @pl.when(i == n_prompt_tiles)
    def _():
        body(hs_ref, pools_ref, attns_ref[...], ys_ref)


def _outproj(hp, pool_p, attn_p, hs, pool_s, attn_s, w_out, g_post):
    mp, d = hp.shape
    ms = hs.shape[0]
    d_pool = pool_p.shape[1]
    n_heads, _, v_head = attn_p.shape
    tm = ROW_TILE
    ntp = mp // tm
    prompt_row = lambda i: (jnp.minimum(i, ntp - 1), 0)
    fixed = lambda i: (0, 0)
    return pl.pallas_call(
        functools.partial(_outproj_kernel, n_prompt_tiles=ntp, d_pool=d_pool),
        out_shape=(jax.ShapeDtypeStruct((mp, d), F32), jax.ShapeDtypeStruct((ms, d), F32)),
        grid=(ntp + 1,),
        in_specs=[
            pl.BlockSpec((tm, d), prompt_row),
            pl.BlockSpec((tm, d_pool), prompt_row),
            pl.BlockSpec((n_heads, tm, v_head), lambda i: (0, jnp.minimum(i, ntp - 1), 0)),
            pl.BlockSpec((ms, d), fixed),
            pl.BlockSpec((ms, d_pool), fixed),
            pl.BlockSpec((ms, n_heads * v_head), fixed),
            pl.BlockSpec(w_out.shape, fixed),
            pl.BlockSpec((1, d), fixed),
        ],
        out_specs=(pl.BlockSpec((tm, d), prompt_row), pl.BlockSpec((ms, d), fixed)),
        compiler_params=_params(1),
        name="outproj",
    )(hp, pool_p, attn_p, hs, pool_s, attn_s, w_out, g_post)


def _rotate_half_columns(w):
    half = QK_ROPE // 2
    return jnp.concatenate([-w[..., half:], w[..., :half]], axis=-1)


def _rope_table(pos):
    half = QK_ROPE // 2
    inv = ROPE_THETA ** (-jnp.arange(half, dtype=F32) / half)
    ang = pos.astype(F32)[:, None] * inv[None, :]
    cos, sin = jnp.cos(ang), jnp.sin(ang)
    return jnp.concatenate([cos, cos, sin, sin], axis=-1)


def kernel(x_prompt, x_sample, cache_kv_latent, cache_k_rope, state_pool, page_table,
           g_ffn1_pre, w1_gate, w1_up, w1_down, g_ffn1_post,
           g_mix_pre, w_in, w_pool, pool_scale, g_q, w_uq, g_kv, w_uk, w_uv, w_out, g_mix_post,
           g_ffn2_pre, w2_gate, w2_up, w2_down, g_ffn2_post):
    depth = w_in.shape[0]
    bp, sp, d = x_prompt.shape
    bs, ts, _ = x_sample.shape
    n_pages = page_table.shape[1]
    page = cache_kv_latent.shape[2]
    past_len = n_pages * page
    kv_lora, n_heads, _ = w_uk.shape[1:]
    q_lora = w_uq.shape[1]
    d_pool = w_pool.shape[1] * w_pool.shape[2]
    assert ts == 1, "the decode kernel handles one new token per sample sequence"
    sm_scale = (QK_NOPE + QK_ROPE) ** -0.5

    cs_p = _rope_table(jnp.arange(sp, dtype=jnp.int32))
    cs_s = jnp.tile(_rope_table(past_len + jnp.arange(ts, dtype=jnp.int32)), (bs, 1))

    hp = x_prompt.reshape(bp * sp, d)
    hs = x_sample.reshape(bs * ts, d)
    outs = [[] for _ in range(6)]
    for l in range(depth):
        row = lambda g: g[l][None, :]
        w_in_l = w_in[l]
        w_in2 = jnp.concatenate(
            [w_in_l, _rotate_half_columns(w_in_l[:, d_pool + q_lora + kv_lora:])], axis=1).astype(BF16)
        wq = w_uq[l].reshape(q_lora, n_heads, QK_NOPE + QK_ROPE)
        w_q2 = jnp.concatenate(
            [wq, _rotate_half_columns(wq[..., QK_NOPE:])], axis=-1).reshape(q_lora, n_heads * HEAD_W).astype(BF16)
        w_kv2 = jnp.concatenate(
            [w_uk[l].reshape(kv_lora, -1), w_uv[l].reshape(kv_lora, -1)], axis=1).astype(BF16)
        w_uk_t = jnp.transpose(w_uk[l], (1, 2, 0)).astype(BF16)
        w_uv_h = jnp.transpose(w_uv[l], (1, 0, 2)).astype(BF16)
        w_pool_l = w_pool[l].astype(BF16)
        w_out_l = w_out[l].astype(BF16)

        hp, hs = _ffn(hp, hs, row(g_ffn1_pre), w1_gate[l], w1_up[l], w1_down[l], row(g_ffn1_post))
        (u_p, ckv_p, kpe_p, q_p, k_p, v_p, u_s, ckv_s, kpe_s, qlat_s, qpe_s) = _proj(
            hp, hs, cs_p, cs_s, row(g_mix_pre), w_in2, row(g_q), w_q2, row(g_kv), w_kv2, w_uk_t,
            seq=sp, n_heads=n_heads)
        pool_p = _pool(u_p, w_pool_l, row(pool_scale), batch=bp, seq=sp)
        attn_p = _attn(q_p, k_p, v_p, batch=bp, seq=sp, sm_scale=sm_scale)
        o_lat = _decode(page_table, jnp.transpose(qlat_s, (1, 0, 2)), jnp.transpose(qpe_s, (1, 0, 2)),
                        ckv_s, kpe_s, cache_kv_latent[l],
                        jnp.swapaxes(cache_k_rope[l], 1, 2),
                        sm_scale=sm_scale)
        state_t = jnp.transpose(state_pool[l], (1, 0, 2))
        pool_s, attn_s = _sample_mix(jnp.transpose(o_lat, (1, 0, 2)), w_uv_h, state_t, u_s,
                                     w_pool_l, row(pool_scale), past_len=past_len)
        hp, hs = _outproj(hp, pool_p, attn_p, hs, pool_s, attn_s, w_out_l, row(g_mix_post))
        hp, hs = _ffn(hp, hs, row(g_ffn2_pre), w2_gate[l], w2_up[l], w2_down[l], row(g_ffn2_post))

        u_p3 = u_p.reshape(bp, sp, d_pool)
        outs[0].append(ckv_p.reshape(bp, sp, kv_lora))
        outs[1].append(kpe_p.reshape(bp, sp, QK_ROPE))
        outs[2].append(u_p3[:, sp - POOL_STATE:])
        outs[3].append(ckv_s.reshape(bs, ts, kv_lora))
        outs[4].append(kpe_s.reshape(bs, ts, QK_ROPE))
        new_rows_t = jnp.transpose(u_s.reshape(bs, ts, d_pool), (1, 0, 2))
        outs[5].append(jnp.transpose(jnp.concatenate([state_t[ts:], new_rows_t], axis=0), (1, 0, 2)))
    return (hp.reshape(bp, sp, d), hs.reshape(bs, ts, d), *[jnp.stack(o) for o in outs])
```

```python
import functools
import math

import jax
import jax.numpy as jnp
from jax import lax
from jax.experimental import pallas as pl
from jax.experimental.pallas import tpu as pltpu

F32 = jnp.float32
BF16 = jnp.bfloat16

EPS = 1e-6
ROPE_THETA = 10000.0
POOL_WINDOWS = (2, 4, 8, 16)
POOL_STATE = max(POOL_WINDOWS) - 1
QK_NOPE = 128
QK_ROPE = 64
V_HEAD = 128
HEAD_W = 2 * 128

V7X_VMEM_BYTES = 64 * 1024 * 1024
VMEM_LIMIT = V7X_VMEM_BYTES - 8 * 1024 * 1024

ROW_TILE = 512
FFN_ROW_TILE = 1024
FF_TILE = 256
NORM_ROWS = 128
PROJ_TILE = 512
POOL_TILE = 512
ATTN_TILE = 512
ATTN_HEADS_IN_FLIGHT = 4
DECODE_CHUNK_PAGES = 32


def _params(n_grid_dims):
    return pltpu.CompilerParams(
        dimension_semantics=("arbitrary",) * n_grid_dims, vmem_limit_bytes=VMEM_LIMIT)


def _rms(x, g):
    return x * lax.rsqrt(jnp.mean(x * x, axis=-1, keepdims=True) + EPS) * g


def _dot(a, b):
    return jnp.dot(a, b, preferred_element_type=F32)


def _dot_nt(a, b):
    return lax.dot_general(a, b, (((1,), (1,)), ((), ())), preferred_element_type=F32)


def _for_row_chunks(rows, fn):
    n_chunks, rem = divmod(rows, NORM_ROWS)
    assert rem == 0

    def step(c, carry):
        fn(pl.ds(pl.multiple_of(c * NORM_ROWS, NORM_ROWS), NORM_ROWS))
        return carry

    lax.fori_loop(0, n_chunks, step, 0, unroll=2 if n_chunks % 2 == 0 else 1)


def _ffn_kernel(xp_ref, xs_ref, gpre_ref, wg_ref, wu_ref, wd_ref, gpost_ref,
                yp_ref, ys_ref, xnp_ref, xns_ref, *, n_ff_tiles):
    last_tile = pl.program_id(0) == pl.num_programs(0) - 1
    first_ff = pl.program_id(1) == 0
    last_ff = pl.program_id(1) == n_ff_tiles - 1
    tm = xp_ref.shape[0]

    def prologue(x_ref, y_ref, xn_ref):
        def chunk(rs):
            xn_ref[rs, :] = _rms(x_ref[rs, :], gpre_ref[...]).astype(BF16)
            y_ref[rs, :] = jnp.zeros((NORM_ROWS, y_ref.shape[1]), F32)

        _for_row_chunks(x_ref.shape[0], chunk)

    def epilogue(x_ref, y_ref):
        def chunk(rs):
            y_ref[rs, :] = x_ref[rs, :] + 0.5 * _rms(y_ref[rs, :], gpost_ref[...])

        _for_row_chunks(x_ref.shape[0], chunk)

    def swiglu_part(xn):
        gate = _dot(xn, wg_ref[...].astype(BF16))
        up = _dot(xn, wu_ref[...].astype(BF16))
        hidden = (gate / (1.0 + jnp.exp(-gate))) * up
        return _dot(hidden.astype(BF16), wd_ref[...].astype(BF16))

    pl.when(first_ff)(lambda: prologue(xp_ref, yp_ref, xnp_ref))
    pl.when(first_ff & last_tile)(lambda: prologue(xs_ref, ys_ref, xns_ref))

    @pl.when(jnp.logical_not(last_tile))
    def _():
        yp_ref[...] += swiglu_part(xnp_ref[...])

    @pl.when(last_tile)
    def _():
        part = swiglu_part(jnp.concatenate([xnp_ref[...], xns_ref[...]], axis=0))
        yp_ref[...] += part[:tm]
        ys_ref[...] += part[tm:]

    pl.when(last_ff)(lambda: epilogue(xp_ref, yp_ref))
    pl.when(last_ff & last_tile)(lambda: epilogue(xs_ref, ys_ref))


def _ffn(xp, xs, g_pre, w_gate, w_up, w_down, g_post):
    mp, d = xp.shape
    ms = xs.shape[0]
    d_ff = w_gate.shape[1]
    tm = FFN_ROW_TILE
    nj = d_ff // FF_TILE
    assert mp % tm == 0 and d_ff % FF_TILE == 0
    prompt_row = lambda i, j: (i, 0)
    fixed = lambda i, j: (0, 0)
    return pl.pallas_call(
        functools.partial(_ffn_kernel, n_ff_tiles=nj),
        out_shape=(jax.ShapeDtypeStruct((mp, d), F32), jax.ShapeDtypeStruct((ms, d), F32)),
        grid=(mp // tm, nj),
        in_specs=[
            pl.BlockSpec((tm, d), prompt_row),
            pl.BlockSpec((ms, d), fixed),
            pl.BlockSpec((1, d), fixed),
            pl.BlockSpec((d, FF_TILE), lambda i, j: (0, j)),
            pl.BlockSpec((d, FF_TILE), lambda i, j: (0, j)),
            pl.BlockSpec((FF_TILE, d), lambda i, j: (j, 0)),
            pl.BlockSpec((1, d), fixed),
        ],
        out_specs=(pl.BlockSpec((tm, d), prompt_row), pl.BlockSpec((ms, d), fixed)),
        scratch_shapes=[pltpu.VMEM((tm, d), BF16), pltpu.VMEM((ms, d), BF16)],
        compiler_params=_params(2),
        name="ffn",
    )(xp, xs, g_pre, w_gate, w_up, w_down, g_post)


def _rope_pairs(slab, cs):
    t = slab * cs
    return t + pltpu.roll(t, QK_ROPE, axis=1)


def _proj_kernel(hp_ref, hs_ref, csp_ref, css_ref, gmix_ref, win_ref, gq_ref, wq_ref, gkv_ref,
                 wkv_ref, wukt_ref,
                 up_ref, ckvp_ref, kpep_ref, q_ref, k_ref, v_ref,
                 us_ref, ckvs_ref, kpes_ref, qlat_ref, qpes_ref,
                 *, n_prompt_tiles, n_heads, d_pool, q_lora, kv_lora):
    i = pl.program_id(0)

    def common(h_ref, cs_ref, u_ref, ckv_ref, kpe_ref):
        hn = _rms(h_ref[...], gmix_ref[...]).astype(BF16)
        z = _dot_nt(hn, win_ref[...])
        u_ref[...] = z[:, :d_pool]
        cqn = _rms(z[:, d_pool:d_pool + q_lora], gq_ref[...]).astype(BF16)
        ckv = _rms(z[:, d_pool + q_lora:d_pool + q_lora + kv_lora], gkv_ref[...])
        ckv_ref[...] = ckv
        cs = cs_ref[...]
        kpe = _rope_pairs(z[:, d_pool + q_lora + kv_lora:], cs)
        kpe_ref[...] = kpe[:, :QK_ROPE]
        qq = _dot(cqn, wq_ref[...])
        return ckv, kpe, qq, cs

    @pl.when(i < n_prompt_tiles)
    def _():
        ckv, kpe, qq, cs = common(hp_ref, csp_ref, up_ref, ckvp_ref, kpep_ref)
        lane = lax.broadcasted_iota(jnp.int32, kpe.shape, 1)
        kpe_pad = jnp.where(lane < QK_ROPE, kpe, 0.0).astype(BF16)
        kv = _dot(ckv.astype(BF16), wkv_ref[...])
        for h in range(n_heads):
            v_lo = n_heads * QK_NOPE + h * V_HEAD
            v_ref[h] = kv[:, v_lo:v_lo + V_HEAD].astype(BF16)
            slab = qq[:, h * HEAD_W:(h + 1) * HEAD_W]
            q_ref[h, :, :QK_NOPE] = slab[:, :QK_NOPE].astype(BF16)
            q_ref[h, :, QK_NOPE:] = _rope_pairs(slab[:, QK_NOPE:], cs).astype(BF16)
            k_ref[h, :, :QK_NOPE] = kv[:, h * QK_NOPE:(h + 1) * QK_NOPE].astype(BF16)
            k_ref[h, :, QK_NOPE:] = kpe_pad

    @pl.when(i == n_prompt_tiles)
    def _():
        _, _, qq, cs = common(hs_ref, css_ref, us_ref, ckvs_ref, kpes_ref)
        for h in range(n_heads):
            slab = qq[:, h * HEAD_W:(h + 1) * HEAD_W]
            qlat_ref[h] = _dot(slab[:, :QK_NOPE].astype(BF16), wukt_ref[h]).astype(BF16)
            qpes_ref[h] = _rope_pairs(slab[:, QK_NOPE:], cs)[:, :QK_ROPE].astype(BF16)


def _proj(hp, hs, cs_p, cs_s, g_mix, w_in2, g_q, w_q2, g_kv, w_kv2, w_uk_t, *, seq, n_heads):
    mp, d = hp.shape
    ms = hs.shape[0]
    tm = PROJ_TILE
    ntp = mp // tm
    tiles_per_seq = seq // tm
    assert mp % tm == 0 and seq % tm == 0
    d_in2 = w_in2.shape[0]
    q_lora = w_q2.shape[0]
    kv_lora = w_kv2.shape[0]
    d_pool = d_in2 - q_lora - kv_lora - 2 * QK_ROPE
    prompt_row = lambda i: (jnp.minimum(i, ntp - 1), 0)
    prompt_head_row = lambda i: (0, jnp.minimum(i, ntp - 1), 0)
    fixed2 = lambda i: (0, 0)
    fixed3 = lambda i: (0, 0, 0)
    whole = lambda a: pl.BlockSpec(a.shape, fixed2 if a.ndim == 2 else fixed3)
    out_shape = (
        jax.ShapeDtypeStruct((mp, d_pool), F32),
        jax.ShapeDtypeStruct((mp, kv_lora), F32),
        jax.ShapeDtypeStruct((mp, QK_ROPE), F32),
        jax.ShapeDtypeStruct((n_heads, mp, HEAD_W), BF16),
        jax.ShapeDtypeStruct((n_heads, mp, HEAD_W), BF16),
        jax.ShapeDtypeStruct((n_heads, mp, V_HEAD), BF16),
        jax.ShapeDtypeStruct((ms, d_pool), F32),
        jax.ShapeDtypeStruct((ms, kv_lora), F32),
        jax.ShapeDtypeStruct((ms, QK_ROPE), F32),
        jax.ShapeDtypeStruct((n_heads, ms, kv_lora), BF16),
        jax.ShapeDtypeStruct((n_heads, ms, QK_ROPE), BF16),
    )
    out_specs = (
        pl.BlockSpec((tm, d_pool), prompt_row),
        pl.BlockSpec((tm, kv_lora), prompt_row),
        pl.BlockSpec((tm, QK_ROPE), prompt_row),
        pl.BlockSpec((n_heads, tm, HEAD_W), prompt_head_row),
        pl.BlockSpec((n_heads, tm, HEAD_W), prompt_head_row),
        pl.BlockSpec((n_heads, tm, V_HEAD), prompt_head_row),
        pl.BlockSpec((ms, d_pool), fixed2),
        pl.BlockSpec((ms, kv_lora), fixed2),
        pl.BlockSpec((ms, QK_ROPE), fixed2),
        pl.BlockSpec((n_heads, ms, kv_lora), fixed3),
        pl.BlockSpec((n_heads, ms, QK_ROPE), fixed3),
    )
    return pl.pallas_call(
        functools.partial(_proj_kernel, n_prompt_tiles=ntp, n_heads=n_heads, d_pool=d_pool,
                          q_lora=q_lora, kv_lora=kv_lora),
        out_shape=out_shape,
        grid=(ntp + 1,),
        in_specs=[
            pl.BlockSpec((tm, d), prompt_row),
            pl.BlockSpec((ms, d), fixed2),
            pl.BlockSpec((tm, 2 * QK_ROPE), lambda i: (jnp.minimum(i, ntp - 1) % tiles_per_seq, 0)),
            whole(cs_s), whole(g_mix), whole(w_in2), whole(g_q), whole(w_q2), whole(g_kv),
            whole(w_kv2), whole(w_uk_t),
        ],
        out_specs=out_specs,
        compiler_params=_params(1),
        name="proj",
    )(hp, hs, cs_p, cs_s, g_mix, w_in2, g_q, w_q2, g_kv, w_kv2, w_uk_t)


def _pool_delta(window_sum, cur, inv_cnt):
    return window_sum * inv_cnt - cur


def _pool_kernel(u_ref, wp_ref, scale_ref, o_ref, ext_ref, *, group):
    i = pl.program_id(1)
    ts = POOL_TILE
    halo = POOL_STATE + 1

    @pl.when(i == 0)
    def _():
        ext_ref[:halo, :] = jnp.zeros((halo, ext_ref.shape[1]), F32)

    @pl.when(i > 0)
    def _():
        ext_ref[:halo, :] = ext_ref[ts:ts + halo, :]

    ext_ref[halo:, :] = u_ref[...]
    pos = i * ts + lax.broadcasted_iota(jnp.int32, (ts, 1), 0)
    for g, w in enumerate(POOL_WINDOWS):
        cols = slice(g * group, (g + 1) * group)
        cur = ext_ref[halo:, cols]
        acc = cur
        for k in range(1, w):
            acc = acc + ext_ref[halo - k:halo - k + ts, cols]
        inv_cnt = 1.0 / jnp.minimum(pos + 1, w).astype(F32)
        delta = _pool_delta(acc, cur, inv_cnt)
        out = _dot(delta.astype(BF16), wp_ref[g])
        o_ref[:, cols] = (out * scale_ref[:, cols]).astype(BF16)


def _pool(u, w_pool, pool_scale, *, batch, seq):
    m, d_pool = u.shape
    ts = POOL_TILE
    nt = seq // ts
    group = d_pool // len(POOL_WINDOWS)
    assert seq % ts == 0 and m == batch * seq
    return pl.pallas_call(
        functools.partial(_pool_kernel, group=group),
        out_shape=jax.ShapeDtypeStruct((m, d_pool), BF16),
        grid=(batch, nt),
        in_specs=[
            pl.BlockSpec((ts, d_pool), lambda b, i: (b * nt + i, 0)),
            pl.BlockSpec(w_pool.shape, lambda b, i: (0, 0, 0)),
            pl.BlockSpec((1, d_pool), lambda b, i: (0, 0)),
        ],
        out_specs=pl.BlockSpec((ts, d_pool), lambda b, i: (b * nt + i, 0)),
        scratch_shapes=[pltpu.VMEM((ts + POOL_STATE + 1, d_pool), F32)],
        compiler_params=_params(2),
        name="pool",
    )(u, w_pool, pool_scale)


def _attn_kernel(q_ref, k_ref, v_ref, o_ref, *, n_heads, n_q_tiles, exp2_scale):
    qi = pl.program_id(1)
    t = ATTN_TILE
    row = lax.broadcasted_iota(jnp.int32, (t, t), 0)
    col = lax.broadcasted_iota(jnp.int32, (t, t), 1)
    causal = row >= col

    def tile(n_past):
        def head(h, carry):
            q = q_ref[h]
            s_diag = jnp.where(causal, _dot_nt(q, k_ref[h, n_past:n_past + t, :]), -jnp.inf)
            m = jnp.max(s_diag, axis=1, keepdims=True)
            if n_past:
                s_past = _dot_nt(q, k_ref[h, :n_past, :])
                m = jnp.maximum(m, jnp.max(s_past, axis=1, keepdims=True))
            p_diag = jnp.exp2((s_diag - m) * exp2_scale)
            l = jnp.sum(p_diag, axis=1, keepdims=True)
            o = _dot(p_diag.astype(BF16), v_ref[h, n_past:n_past + t, :])
            if n_past:
                p_past = jnp.exp2((s_past - m) * exp2_scale)
                l = l + jnp.sum(p_past, axis=1, keepdims=True)
                o = o + _dot(p_past.astype(BF16), v_ref[h, :n_past, :])
            o_ref[h] = (o / l).astype(BF16)
            return carry

        lax.fori_loop(0, n_heads, head, 0, unroll=ATTN_HEADS_IN_FLIGHT)

    for kq in range(n_q_tiles):
        pl.when(qi == kq)(functools.partial(tile, kq * t))


def _attn(q, k, v, *, batch, seq, sm_scale):
    n_heads, m, _ = q.shape
    t = ATTN_TILE
    nq = seq // t
    assert seq % t == 0
    return pl.pallas_call(
        functools.partial(_attn_kernel, n_heads=n_heads, n_q_tiles=nq,
                          exp2_scale=sm_scale * math.log2(math.e)),
        out_shape=jax.ShapeDtypeStruct((n_heads, m, V_HEAD), BF16),
        grid=(batch, nq),
        in_specs=[
            pl.BlockSpec((n_heads, t, HEAD_W), lambda b, i: (0, b * nq + i, 0)),
            pl.BlockSpec((n_heads, seq, HEAD_W), lambda b, i: (0, b, 0)),
            pl.BlockSpec((n_heads, seq, V_HEAD), lambda b, i: (0, b, 0)),
        ],
        out_specs=pl.BlockSpec((n_heads, t, V_HEAD), lambda b, i: (0, b * nq + i, 0)),
        compiler_params=_params(2),
        name="attn",
    )(q, k, v)


def _decode_kernel(pt_ref, qlat_ref, qpe_ref, ckvs_ref, kpes_ref, kv_hbm, rope_hbm, o_ref,
                   kv_buf, rope_buf, kv_sem, rope_sem, *, n_pages, sm_scale):
    b = pl.program_id(0)
    n_samples = pl.num_programs(0)
    slot = b % 2

    def page_copies(sample, sl):
        copies = []
        for i in range(n_pages):
            page_id = pt_ref[sample * n_pages + i]
            copies.append(pltpu.make_async_copy(kv_hbm.at[page_id], kv_buf.at[sl, i], kv_sem.at[sl]))
            copies.append(pltpu.make_async_copy(rope_hbm.at[page_id], rope_buf.at[sl, i], rope_sem.at[sl]))
        return copies

    @pl.when(b == 0)
    def _():
        for c in page_copies(0, 0):
            c.start()

    @pl.when(b + 1 < n_samples)
    def _():
        for c in page_copies(b + 1, 1 - slot):
            c.start()

    for c in page_copies(b, slot):
        c.wait()

    qlat = qlat_ref[b]
    qpe = qpe_ref[b]
    own_ckv = ckvs_ref[pl.ds(b, 1), :].astype(BF16).astype(F32)
    own_kpe = kpes_ref[pl.ds(b, 1), :].astype(BF16).astype(F32)
    s_own = (jnp.sum(qlat.astype(F32) * own_ckv, axis=1, keepdims=True)
             + jnp.sum(qpe.astype(F32) * own_kpe, axis=1, keepdims=True)) * sm_scale

    chunk = DECODE_CHUNK_PAGES
    ckv_parts, s_parts = [], []
    for c0 in range(0, n_pages, chunk):
        ckv = jnp.concatenate([kv_buf[slot, i] for i in range(c0, c0 + chunk)], axis=0).astype(BF16)
        kpe_t = jnp.concatenate([rope_buf[slot, i] for i in range(c0, c0 + chunk)], axis=1).astype(BF16)
        ckv_parts.append(ckv)
        s_parts.append((_dot_nt(qlat, ckv) + _dot(qpe, kpe_t)) * sm_scale)
    m = s_own
    for s in s_parts:
        m = jnp.maximum(m, jnp.max(s, axis=1, keepdims=True))
    p_own = jnp.exp(s_own - m)
    l = p_own
    acc = p_own.astype(BF16).astype(F32) * own_ckv
    for s, ckv in zip(s_parts, ckv_parts):
        p = jnp.exp(s - m)
        l = l + jnp.sum(p, axis=1, keepdims=True)
        acc = acc + _dot(p.astype(BF16), ckv)
    o_ref[b] = acc / l


def _decode(page_table, q_lat, q_pe, ckv_own, kpe_own, cache_kv, cache_rope_t, *, sm_scale):
    bs, n_heads, kv_lora = q_lat.shape
    n_pages = page_table.shape[1]
    page = cache_kv.shape[1]
    assert n_pages % DECODE_CHUNK_PAGES == 0
    whole = lambda a: pl.BlockSpec(a.shape, lambda b, pt, nd=a.ndim: (0,) * nd)
    grid_spec = pltpu.PrefetchScalarGridSpec(
        num_scalar_prefetch=1,
        grid=(bs,),
        in_specs=[whole(q_lat), whole(q_pe), whole(ckv_own), whole(kpe_own),
                  pl.BlockSpec(memory_space=pl.ANY), pl.BlockSpec(memory_space=pl.ANY)],
        out_specs=pl.BlockSpec((bs, n_heads, kv_lora), lambda b, pt: (0, 0, 0)),
        scratch_shapes=[
            pltpu.VMEM((2, n_pages, page, kv_lora), F32),
            pltpu.VMEM((2, n_pages, QK_ROPE, page), F32),
            pltpu.SemaphoreType.DMA((2,)),
            pltpu.SemaphoreType.DMA((2,)),
        ],
    )
    return pl.pallas_call(
        functools.partial(_decode_kernel, n_pages=n_pages, sm_scale=sm_scale),
        out_shape=jax.ShapeDtypeStruct((bs, n_heads, kv_lora), F32),
        grid_spec=grid_spec,
        compiler_params=_params(1),
        name="decode",
    )(page_table.reshape(-1), q_lat, q_pe, ckv_own, kpe_own, cache_kv, cache_rope_t)


def _sample_kernel(olat_ref, wuv_ref, sp_ref, us_ref, wp_ref, scale_ref, pool_ref, attn_ref,
                   *, n_heads, group, past_len):
    for h in range(n_heads):
        attn_ref[:, h * V_HEAD:(h + 1) * V_HEAD] = _dot(olat_ref[h].astype(BF16), wuv_ref[h]).astype(BF16)
    for g, w in enumerate(POOL_WINDOWS):
        cols = slice(g * group, (g + 1) * group)
        cur = us_ref[:, cols]
        acc = cur
        for k in range(1, w):
            acc = acc + sp_ref[POOL_STATE - k, :, cols]
        delta = _pool_delta(acc, cur, 1.0 / min(past_len + 1, w))
        out = _dot(delta.astype(BF16), wp_ref[g])
        pool_ref[:, cols] = (out * scale_ref[:, cols]).astype(BF16)


def _sample_mix(o_lat, w_uv_h, state_pool, u_s, w_pool, pool_scale, *, past_len):
    n_heads, ms, _ = o_lat.shape
    d_pool = u_s.shape[1]
    return pl.pallas_call(
        functools.partial(_sample_kernel, n_heads=n_heads, group=d_pool // len(POOL_WINDOWS),
                          past_len=past_len),
        out_shape=(jax.ShapeDtypeStruct((ms, d_pool), BF16),
                   jax.ShapeDtypeStruct((ms, n_heads * V_HEAD), BF16)),
        compiler_params=pltpu.CompilerParams(vmem_limit_bytes=VMEM_LIMIT),
        name="sample_mix",
    )(o_lat, w_uv_h, state_pool, u_s, w_pool, pool_scale)


def _outproj_kernel(hp_ref, poolp_ref, attnp_ref, hs_ref, pools_ref, attns_ref, wo_ref, g_ref,
                    yp_ref, ys_ref, *, n_prompt_tiles, d_pool):
    i = pl.program_id(0)

    def body(h_ref, pool_ref, attn, y_ref):
        mix = _dot(pool_ref[...], wo_ref[:d_pool, :]) + _dot(attn, wo_ref[d_pool:, :])
        y_ref[...] = h_ref[...] + _rms(mix, g_ref[...])

    @pl.when(i < n_prompt_tiles)
    def _():
        heads = [attnp_ref[h] for h in range(attnp_ref.shape[0])]
        body(hp_ref, poolp_ref, jnp.concatenate(heads, axis=1), yp_ref)

    @pl.when(i == n_prompt_tiles)
    def _():
        body(hs_ref, pools_ref, attns_ref[...], ys_ref)


def _outproj(hp, pool_p, attn_p, hs, pool_s, attn_s, w_out, g_post):
    mp, d = hp.shape
    ms = hs.shape[0]
    d_pool = pool_p.shape[1]
    n_heads, _, v_head = attn_p.shape
    tm = ROW_TILE
    ntp = mp // tm
    prompt_row = lambda i: (jnp.minimum(i, ntp - 1), 0)
    fixed = lambda i: (0, 0)
    return pl.pallas_call(
        functools.partial(_outproj_kernel, n_prompt_tiles=ntp, d_pool=d_pool),
        out_shape=(jax.ShapeDtypeStruct((mp, d), F32), jax.ShapeDtypeStruct((ms, d), F32)),
        grid=(ntp + 1,),
        in_specs=[
            pl.BlockSpec((tm, d), prompt_row),
            pl.BlockSpec((tm, d_pool), prompt_row),
            pl.BlockSpec((n_heads, tm, v_head), lambda i: (0, jnp.minimum(i, ntp - 1), 0)),
            pl.BlockSpec((ms, d), fixed),
            pl.BlockSpec((ms, d_pool), fixed),
            pl.BlockSpec((ms, n_heads * v_head), fixed),
            pl.BlockSpec(w_out.shape, fixed),
            pl.BlockSpec((1, d), fixed),
        ],
        out_specs=(pl.BlockSpec((tm, d), prompt_row), pl.BlockSpec((ms, d), fixed)),
        compiler_params=_params(1),
        name="outproj",
    )(hp, pool_p, attn_p, hs, pool_s, attn_s, w_out, g_post)


def _rotate_half_columns(w):
    half = QK_ROPE // 2
    return jnp.concatenate([-w[..., half:], w[..., :half]], axis=-1)


def _rope_table(pos):
    half = QK_ROPE // 2
    inv = ROPE_THETA ** (-jnp.arange(half, dtype=F32) / half)
    ang = pos.astype(F32)[:, None] * inv[None, :]
    cos, sin = jnp.cos(ang), jnp.sin(ang)
    return jnp.concatenate([cos, cos, sin, sin], axis=-1)


def kernel(x_prompt, x_sample, cache_kv_latent, cache_k_rope, state_pool, page_table,
           g_ffn1_pre, w1_gate, w1_up, w1_down, g_ffn1_post,
           g_mix_pre, w_in, w_pool, pool_scale, g_q, w_uq, g_kv, w_uk, w_uv, w_out, g_mix_post,
           g_ffn2_pre, w2_gate, w2_up, w2_down, g_ffn2_post):
    depth = w_in.shape[0]
    bp, sp, d = x_prompt.shape
    bs, ts, _ = x_sample.shape
    n_pages = page_table.shape[1]
    page = cache_kv_latent.shape[2]
    past_len = n_pages * page
    kv_lora, n_heads, _ = w_uk.shape[1:]
    q_lora = w_uq.shape[1]
    d_pool = w_pool.shape[1] * w_pool.shape[2]
    assert ts == 1, "the decode kernel handles one new token per sample sequence"
    sm_scale = (QK_NOPE + QK_ROPE) ** -0.5

    cs_p = _rope_table(jnp.arange(sp, dtype=jnp.int32))
    cs_s = jnp.tile(_rope_table(past_len + jnp.arange(ts, dtype=jnp.int32)), (bs, 1))

    hp = x_prompt.reshape(bp * sp, d)
    hs = x_sample.reshape(bs * ts, d)
    outs = [[] for _ in range(6)]
    for l in range(depth):
        row = lambda g: g[l][None, :]
        w_in_t = jnp.swapaxes(w_in[l], 0, 1)
        w_rope_t = jnp.swapaxes(_rotate_half_columns(w_in[l][:, d_pool + q_lora + kv_lora:]), 0, 1)
        w_in2 = jnp.concatenate([w_in_t, w_rope_t], axis=0).astype(BF16)
        wq = w_uq[l].reshape(q_lora, n_heads, QK_NOPE + QK_ROPE)
        w_q2 = jnp.concatenate(
            [wq, _rotate_half_columns(wq[..., QK_NOPE:])], axis=-1).reshape(q_lora, n_heads * HEAD_W).astype(BF16)
        w_kv2 = jnp.concatenate(
            [w_uk[l].reshape(kv_lora, -1), w_uv[l].reshape(kv_lora, -1)], axis=1).astype(BF16)
        w_uk_t = jnp.transpose(w_uk[l], (1, 2, 0)).astype(BF16)
        w_uv_h = jnp.transpose(w_uv[l], (1, 0, 2)).astype(BF16)
        w_pool_l = w_pool[l].astype(BF16)
        w_out_l = w_out[l].astype(BF16)

        hp, hs = _ffn(hp, hs, row(g_ffn1_pre), w1_gate[l], w1_up[l], w1_down[l], row(g_ffn1_post))
        (u_p, ckv_p, kpe_p, q_p, k_p, v_p, u_s, ckv_s, kpe_s, qlat_s, qpe_s) = _proj(
            hp, hs, cs_p, cs_s, row(g_mix_pre), w_in2, row(g_q), w_q2, row(g_kv), w_kv2, w_uk_t,
            seq=sp, n_heads=n_heads)
        pool_p = _pool(u_p, w_pool_l, row(pool_scale), batch=bp, seq=sp)
        attn_p = _attn(q_p, k_p, v_p, batch=bp, seq=sp, sm_scale=sm_scale)
        o_lat = _decode(page_table, jnp.transpose(qlat_s, (1, 0, 2)), jnp.transpose(qpe_s, (1, 0, 2)),
                        ckv_s, kpe_s, cache_kv_latent[l],
                        jnp.swapaxes(cache_k_rope[l], 1, 2),
                        sm_scale=sm_scale)
        state_t = jnp.transpose(state_pool[l], (1, 0, 2))
        pool_s, attn_s = _sample_mix(jnp.transpose(o_lat, (1, 0, 2)), w_uv_h, state_t, u_s,
                                     w_pool_l, row(pool_scale), past_len=past_len)
        hp, hs = _outproj(hp, pool_p, attn_p, hs, pool_s, attn_s, w_out_l, row(g_mix_post))
        hp, hs = _ffn(hp, hs, row(g_ffn2_pre), w2_gate[l], w2_up[l], w2_down[l], row(g_ffn2_post))

        u_p3 = u_p.reshape(bp, sp, d_pool)
        outs[0].append(ckv_p.reshape(bp, sp, kv_lora))
        outs[1].append(kpe_p.reshape(bp, sp, QK_ROPE))
        outs[2].append(u_p3[:, sp - POOL_STATE:])
        outs[3].append(ckv_s.reshape(bs, ts, kv_lora))
        outs[4].append(kpe_s.reshape(bs, ts, QK_ROPE))
        new_rows_t = jnp.transpose(u_s.reshape(bs, ts, d_pool), (1, 0, 2))
        outs[5].append(jnp.transpose(jnp.concatenate([state_t[ts:], new_rows_t], axis=0), (1, 0, 2)))
    return (hp.reshape(bp, sp, d), hs.reshape(bs, ts, d), *[jnp.stack(o) for o in outs])
```

```python
import functools
import math

import jax
import jax.numpy as jnp
from jax import lax
from jax.experimental import pallas as pl
from jax.experimental.pallas import tpu as pltpu

F32 = jnp.float32
BF16 = jnp.bfloat16

EPS = 1e-6
ROPE_THETA = 10000.0
POOL_WINDOWS = (2, 4, 8, 16)
POOL_STATE = max(POOL_WINDOWS) - 1
QK_NOPE = 128
QK_ROPE = 64
V_HEAD = 128
HEAD_W = 2 * 128

V7X_VMEM_BYTES = 64 * 1024 * 1024
VMEM_LIMIT = V7X_VMEM_BYTES - 8 * 1024 * 1024

ROW_TILE = 512
FFN_ROW_TILE = 1024
FF_TILE = 256
NORM_ROWS = 128
PROJ_TILE = 512
POOL_TILE = 512
ATTN_TILE = 512
ATTN_HEADS_IN_FLIGHT = 4
DECODE_CHUNK_PAGES = 32


def _params(n_grid_dims):
    return pltpu.CompilerParams(
        dimension_semantics=("arbitrary",) * n_grid_dims, vmem_limit_bytes=VMEM_LIMIT)


def _rms(x, g):
    return x * lax.rsqrt(jnp.mean(x * x, axis=-1, keepdims=True) + EPS) * g


def _dot(a, b):
    return jnp.dot(a, b, preferred_element_type=F32)


def _dot_nt(a, b):
    return lax.dot_general(a, b, (((1,), (1,)), ((), ())), preferred_element_type=F32)


def _for_row_chunks(rows, fn):
    n_chunks, rem = divmod(rows, NORM_ROWS)
    assert rem == 0

    def step(c, carry):
        fn(pl.ds(pl.multiple_of(c * NORM_ROWS, NORM_ROWS), NORM_ROWS))
        return carry

    lax.fori_loop(0, n_chunks, step, 0, unroll=2 if n_chunks % 2 == 0 else 1)


def _row_inv_rms(x):
    return lax.rsqrt(jnp.mean(x * x, axis=-1, keepdims=True) + EPS)


def _ffn_kernel(xp_ref, xs_ref, gpre_ref, wg0_ref, wg1_ref, wu0_ref, wu1_ref, wd_ref, gpost_ref,
                yp_ref, ys_ref, xnp_ref, xns_ref, *, n_ff_tiles):
    last_tile = pl.program_id(0) == pl.num_programs(0) - 1
    first_ff = pl.program_id(1) == 0
    last_ff = pl.program_id(1) == n_ff_tiles - 1
    tm = xp_ref.shape[0]

    def prologue(x_ref, y_ref, xn_ref):
        def chunk(rs):
            inv = _row_inv_rms(x_ref[rs, :])
            xn_ref[rs, :] = (x_ref[rs, :] * inv * gpre_ref[...]).astype(BF16)
            y_ref[rs, :] = jnp.zeros((NORM_ROWS, y_ref.shape[1]), F32)

        _for_row_chunks(x_ref.shape[0], chunk)

    def epilogue(x_ref, y_ref):
        half_gain = 0.5 * gpost_ref[...]

        def chunk(rs):
            inv = _row_inv_rms(y_ref[rs, :])
            y_ref[rs, :] = x_ref[rs, :] + y_ref[rs, :] * inv * half_gain

        _for_row_chunks(x_ref.shape[0], chunk)

    def swiglu_part(xn):
        w_gate = jnp.concatenate([wg0_ref[...].astype(BF16), wg1_ref[...].astype(BF16)], axis=0)
        w_up = jnp.concatenate([wu0_ref[...].astype(BF16), wu1_ref[...].astype(BF16)], axis=0)
        gate = _dot(xn, w_gate)
        up = _dot(xn, w_up)
        hidden = (gate / (1.0 + jnp.exp(-gate))) * up
        return _dot(hidden.astype(BF16), wd_ref[...].astype(BF16))

    pl.when(first_ff)(lambda: prologue(xp_ref, yp_ref, xnp_ref))
    pl.when(first_ff & last_tile)(lambda: prologue(xs_ref, ys_ref, xns_ref))

    @pl.when(jnp.logical_not(last_tile))
    def _():
        yp_ref[...] += swiglu_part(xnp_ref[...])

    @pl.when(last_tile)
    def _():
        part = swiglu_part(jnp.concatenate([xnp_ref[...], xns_ref[...]], axis=0))
        yp_ref[...] += part[:tm]
        ys_ref[...] += part[tm:]

    pl.when(last_ff)(lambda: epilogue(xp_ref, yp_ref))
    pl.when(last_ff & last_tile)(lambda: epilogue(xs_ref, ys_ref))


def _ffn(xp, xs, g_pre, w_gate, w_up, w_down, g_post):
    mp, d = xp.shape
    ms = xs.shape[0]
    d_ff = w_gate.shape[1]
    tm = FFN_ROW_TILE
    nj = d_ff // FF_TILE
    assert mp % tm == 0 and d_ff % FF_TILE == 0
    prompt_row = lambda i, j: (i, 0)
    fixed = lambda i, j: (0, 0)
    return pl.pallas_call(
        functools.partial(_ffn_kernel, n_ff_tiles=nj),
        out_shape=(jax.ShapeDtypeStruct((mp, d), F32), jax.ShapeDtypeStruct((ms, d), F32)),
        grid=(mp // tm, nj),
        in_specs=[
            pl.BlockSpec((tm, d), prompt_row),
            pl.BlockSpec((ms, d), fixed),
            pl.BlockSpec((1, d), fixed),
            pl.BlockSpec((d // 2, FF_TILE), lambda i, j: (0, j)),
            pl.BlockSpec((d // 2, FF_TILE), lambda i, j: (1, j)),
            pl.BlockSpec((d // 2, FF_TILE), lambda i, j: (0, j)),
            pl.BlockSpec((d // 2, FF_TILE), lambda i, j: (1, j)),
            pl.BlockSpec((FF_TILE, d), lambda i, j: (j, 0)),
            pl.BlockSpec((1, d), fixed),
        ],
        out_specs=(pl.BlockSpec((tm, d), prompt_row), pl.BlockSpec((ms, d), fixed)),
        scratch_shapes=[pltpu.VMEM((tm, d), BF16), pltpu.VMEM((ms, d), BF16)],
        compiler_params=_params(2),
        name="ffn",
    )(xp, xs, g_pre, w_gate, w_gate, w_up, w_up, w_down, g_post)


def _rope_pairs(slab, cs):
    t = slab * cs
    return t + pltpu.roll(t, QK_ROPE, axis=1)


def _proj_kernel(hp_ref, hs_ref, csp_ref, css_ref, gmix_ref, win_ref, gq_ref, wq_ref, gkv_ref,
                 wkv_ref, wukt_ref,
                 up_ref, ckvp_ref, kpep_ref, q_ref, k_ref, v_ref,
                 us_ref, ckvs_ref, kpes_ref, qlat_ref, qpes_ref,
                 *, n_prompt_tiles, n_heads, d_pool, q_lora, kv_lora):
    i = pl.program_id(0)

    def common(h_ref, cs_ref, u_ref, ckv_ref, kpe_ref):
        hn = _rms(h_ref[...], gmix_ref[...]).astype(BF16)
        z = _dot_nt(hn, win_ref[...])
        u_ref[...] = z[:, :d_pool]
        cqn = _rms(z[:, d_pool:d_pool + q_lora], gq_ref[...]).astype(BF16)
        ckv = _rms(z[:, d_pool + q_lora:d_pool + q_lora + kv_lora], gkv_ref[...])
        ckv_ref[...] = ckv
        cs = cs_ref[...]
        kpe = _rope_pairs(z[:, d_pool + q_lora + kv_lora:], cs)
        kpe_ref[...] = kpe[:, :QK_ROPE]
        qq = _dot(cqn, wq_ref[...])
        return ckv, kpe, qq, cs

    @pl.when(i < n_prompt_tiles)
    def _():
        ckv, kpe, qq, cs = common(hp_ref, csp_ref, up_ref, ckvp_ref, kpep_ref)
        lane = lax.broadcasted_iota(jnp.int32, kpe.shape, 1)
        kpe_pad = jnp.where(lane < QK_ROPE, kpe, 0.0).astype(BF16)
        kv = _dot(ckv.astype(BF16), wkv_ref[...])
        for h in range(n_heads):
            v_lo = n_heads * QK_NOPE + h * V_HEAD
            v_ref[h] = kv[:, v_lo:v_lo + V_HEAD].astype(BF16)
            slab = qq[:, h * HEAD_W:(h + 1) * HEAD_W]
            q_ref[h, :, :QK_NOPE] = slab[:, :QK_NOPE].astype(BF16)
            q_ref[h, :, QK_NOPE:] = _rope_pairs(slab[:, QK_NOPE:], cs).astype(BF16)
            k_ref[h, :, :QK_NOPE] = kv[:, h * QK_NOPE:(h + 1) * QK_NOPE].astype(BF16)
            k_ref[h, :, QK_NOPE:] = kpe_pad

    @pl.when(i == n_prompt_tiles)
    def _():
        _, _, qq, cs = common(hs_ref, css_ref, us_ref, ckvs_ref, kpes_ref)
        for h in range(n_heads):
            slab = qq[:, h * HEAD_W:(h + 1) * HEAD_W]
            qlat_ref[h] = _dot(slab[:, :QK_NOPE].astype(BF16), wukt_ref[h]).astype(BF16)
            qpes_ref[h] = _rope_pairs(slab[:, QK_NOPE:], cs)[:, :QK_ROPE].astype(BF16)


def _proj(hp, hs, cs_p, cs_s, g_mix, w_in2, g_q, w_q2, g_kv, w_kv2, w_uk_t, *, seq, n_heads):
    mp, d = hp.shape
    ms = hs.shape[0]
    tm = PROJ_TILE
    ntp = mp // tm
    tiles_per_seq = seq // tm
    assert mp % tm == 0 and seq % tm == 0
    d_in2 = w_in2.shape[0]
    q_lora = w_q2.shape[0]
    kv_lora = w_kv2.shape[0]
    d_pool = d_in2 - q_lora - kv_lora - 2 * QK_ROPE
    prompt_row = lambda i: (jnp.minimum(i, ntp - 1), 0)
    prompt_head_row = lambda i: (0, jnp.minimum(i, ntp - 1), 0)
    fixed2 = lambda i: (0, 0)
    fixed3 = lambda i: (0, 0, 0)
    whole = lambda a: pl.BlockSpec(a.shape, fixed2 if a.ndim == 2 else fixed3)
    out_shape = (
        jax.ShapeDtypeStruct((mp, d_pool), F32),
        jax.ShapeDtypeStruct((mp, kv_lora), F32),
        jax.ShapeDtypeStruct((mp, QK_ROPE), F32),
        jax.ShapeDtypeStruct((n_heads, mp, HEAD_W), BF16),
        jax.ShapeDtypeStruct((n_heads, mp, HEAD_W), BF16),
        jax.ShapeDtypeStruct((n_heads, mp, V_HEAD), BF16),
        jax.ShapeDtypeStruct((ms, d_pool), F32),
        jax.ShapeDtypeStruct((ms, kv_lora), F32),
        jax.ShapeDtypeStruct((ms, QK_ROPE), F32),
        jax.ShapeDtypeStruct((n_heads, ms, kv_lora), BF16),
        jax.ShapeDtypeStruct((n_heads, ms, QK_ROPE), BF16),
    )
    out_specs = (
        pl.BlockSpec((tm, d_pool), prompt_row),
        pl.BlockSpec((tm, kv_lora), prompt_row),
        pl.BlockSpec((tm, QK_ROPE), prompt_row),
        pl.BlockSpec((n_heads, tm, HEAD_W), prompt_head_row),
        pl.BlockSpec((n_heads, tm, HEAD_W), prompt_head_row),
        pl.BlockSpec((n_heads, tm, V_HEAD), prompt_head_row),
        pl.BlockSpec((ms, d_pool), fixed2),
        pl.BlockSpec((ms, kv_lora), fixed2),
        pl.BlockSpec((ms, QK_ROPE), fixed2),
        pl.BlockSpec((n_heads, ms, kv_lora), fixed3),
        pl.BlockSpec((n_heads, ms, QK_ROPE), fixed3),
    )
    return pl.pallas_call(
        functools.partial(_proj_kernel, n_prompt_tiles=ntp, n_heads=n_heads, d_pool=d_pool,
                          q_lora=q_lora, kv_lora=kv_lora),
        out_shape=out_shape,
        grid=(ntp + 1,),
        in_specs=[
            pl.BlockSpec((tm, d), prompt_row),
            pl.BlockSpec((ms, d), fixed2),
            pl.BlockSpec((tm, 2 * QK_ROPE), lambda i: (jnp.minimum(i, ntp - 1) % tiles_per_seq, 0)),
            whole(cs_s), whole(g_mix), whole(w_in2), whole(g_q), whole(w_q2), whole(g_kv),
            whole(w_kv2), whole(w_uk_t),
        ],
        out_specs=out_specs,
        compiler_params=_params(1),
        name="proj",
    )(hp, hs, cs_p, cs_s, g_mix, w_in2, g_q, w_q2, g_kv, w_kv2, w_uk_t)


def _pool_delta(window_sum, cur, inv_cnt):
    return window_sum * inv_cnt - cur


def _pool_kernel(u_ref, wp_ref, scale_ref, o_ref, ext_ref, *, group):
    i = pl.program_id(1)
    ts = POOL_TILE
    halo = POOL_STATE + 1

    @pl.when(i == 0)
    def _():
        ext_ref[:halo, :] = jnp.zeros((halo, ext_ref.shape[1]), F32)

    @pl.when(i > 0)
    def _():
        ext_ref[:halo, :] = ext_ref[ts:ts + halo, :]

    ext_ref[halo:, :] = u_ref[...]
    pos = i * ts + lax.broadcasted_iota(jnp.int32, (ts, 1), 0)
    for g, w in enumerate(POOL_WINDOWS):
        cols = slice(g * group, (g + 1) * group)
        cur = ext_ref[halo:, cols]
        acc = cur
        for k in range(1, w):
            acc = acc + ext_ref[halo - k:halo - k + ts, cols]
        inv_cnt = 1.0 / jnp.minimum(pos + 1, w).astype(F32)
        delta = _pool_delta(acc, cur, inv_cnt)
        out = _dot(delta.astype(BF16), wp_ref[g])
        o_ref[:, cols] = (out * scale_ref[:, cols]).astype(BF16)


def _pool(u, w_pool, pool_scale, *, batch, seq):
    m, d_pool = u.shape
    ts = POOL_TILE
    nt = seq // ts
    group = d_pool // len(POOL_WINDOWS)
    assert seq % ts == 0 and m == batch * seq
    return pl.pallas_call(
        functools.partial(_pool_kernel, group=group),
        out_shape=jax.ShapeDtypeStruct((m, d_pool), BF16),
        grid=(batch, nt),
        in_specs=[
            pl.BlockSpec((ts, d_pool), lambda b, i: (b * nt + i, 0)),
            pl.BlockSpec(w_pool.shape, lambda b, i: (0, 0, 0)),
            pl.BlockSpec((1, d_pool), lambda b, i: (0, 0)),
        ],
        out_specs=pl.BlockSpec((ts, d_pool), lambda b, i: (b * nt + i, 0)),
        scratch_shapes=[pltpu.VMEM((ts + POOL_STATE + 1, d_pool), F32)],
        compiler_params=_params(2),
        name="pool",
    )(u, w_pool, pool_scale)


def _attn_kernel(q_ref, k_ref, v_ref, o_ref, *, n_heads, n_q_tiles, exp2_scale):
    qi = pl.program_id(1)
    t = ATTN_TILE
    row = lax.broadcasted_iota(jnp.int32, (t, t), 0)
    col = lax.broadcasted_iota(jnp.int32, (t, t), 1)
    causal = row >= col

    def tile(n_past):
        def head(h, carry):
            q = q_ref[h]
            s_diag = jnp.where(causal, _dot_nt(q, k_ref[h, n_past:n_past + t, :]), -jnp.inf)
            m = jnp.max(s_diag, axis=1, keepdims=True)
            if n_past:
                s_past = _dot_nt(q, k_ref[h, :n_past, :])
                m = jnp.maximum(m, jnp.max(s_past, axis=1, keepdims=True))
            p_diag = jnp.exp2((s_diag - m) * exp2_scale)
            l = jnp.sum(p_diag, axis=1, keepdims=True)
            o = _dot(p_diag.astype(BF16), v_ref[h, n_past:n_past + t, :])
            if n_past:
                p_past = jnp.exp2((s_past - m) * exp2_scale)
                l = l + jnp.sum(p_past, axis=1, keepdims=True)
                o = o + _dot(p_past.astype(BF16), v_ref[h, :n_past, :])
            o_ref[h] = (o / l).astype(BF16)
            return carry

        lax.fori_loop(0, n_heads, head, 0, unroll=ATTN_HEADS_IN_FLIGHT)

    for kq in range(n_q_tiles):
        pl.when(qi == kq)(functools.partial(tile, kq * t))


def _attn(q, k, v, *, batch, seq, sm_scale):
    n_heads, m, _ = q.shape
    t = ATTN_TILE
    nq = seq // t
    assert seq % t == 0
    return pl.pallas_call(
        functools.partial(_attn_kernel, n_heads=n_heads, n_q_tiles=nq,
                          exp2_scale=sm_scale * math.log2(math.e)),
        out_shape=jax.ShapeDtypeStruct((n_heads, m, V_HEAD), BF16),
        grid=(batch, nq),
        in_specs=[
            pl.BlockSpec((n_heads, t, HEAD_W), lambda b, i: (0, b * nq + i, 0)),
            pl.BlockSpec((n_heads, seq, HEAD_W), lambda b, i: (0, b, 0)),
            pl.BlockSpec((n_heads, seq, V_HEAD), lambda b, i: (0, b, 0)),
        ],
        out_specs=pl.BlockSpec((n_heads, t, V_HEAD), lambda b, i: (0, b * nq + i, 0)),
        compiler_params=_params(2),
        name="attn",
    )(q, k, v)


def _decode_kernel(pt_ref, qlat_ref, qpe_ref, ckvs_ref, kpes_ref, kv_hbm, rope_hbm, o_ref,
                   kv_buf, rope_buf, kv_sem, rope_sem, *, n_pages, sm_scale):
    b = pl.program_id(0)
    n_samples = pl.num_programs(0)
    slot = b % 2

    def page_copies(sample, sl):
        copies = []
        for i in range(n_pages):
            page_id = pt_ref[sample * n_pages + i]
            copies.append(pltpu.make_async_copy(kv_hbm.at[page_id], kv_buf.at[sl, i], kv_sem.at[sl]))
            copies.append(pltpu.make_async_copy(rope_hbm.at[page_id], rope_buf.at[sl, i], rope_sem.at[sl]))
        return copies

    @pl.when(b == 0)
    def _():
        for c in page_copies(0, 0):
            c.start()

    @pl.when(b + 1 < n_samples)
    def _():
        for c in page_copies(b + 1, 1 - slot):
            c.start()

    for c in page_copies(b, slot):
        c.wait()

    qlat = qlat_ref[b]
    qpe = qpe_ref[b]
    own_ckv = ckvs_ref[pl.ds(b, 1), :].astype(BF16).astype(F32)
    own_kpe = kpes_ref[pl.ds(b, 1), :].astype(BF16).astype(F32)
    s_own = (jnp.sum(qlat.astype(F32) * own_ckv, axis=1, keepdims=True)
             + jnp.sum(qpe.astype(F32) * own_kpe, axis=1, keepdims=True)) * sm_scale

    chunk = DECODE_CHUNK_PAGES
    ckv_parts, s_parts = [], []
    for c0 in range(0, n_pages, chunk):
        ckv = jnp.concatenate([kv_buf[slot, i] for i in range(c0, c0 + chunk)], axis=0).astype(BF16)
        kpe_t = jnp.concatenate([rope_buf[slot, i] for i in range(c0, c0 + chunk)], axis=1).astype(BF16)
        ckv_parts.append(ckv)
        s_parts.append((_dot_nt(qlat, ckv) + _dot(qpe, kpe_t)) * sm_scale)
    m = s_own
    for s in s_parts:
        m = jnp.maximum(m, jnp.max(s, axis=1, keepdims=True))
    p_own = jnp.exp(s_own - m)
    l = p_own
    acc = p_own.astype(BF16).astype(F32) * own_ckv
    for s, ckv in zip(s_parts, ckv_parts):
        p = jnp.exp(s - m)
        l = l + jnp.sum(p, axis=1, keepdims=True)
        acc = acc + _dot(p.astype(BF16), ckv)
    o_ref[b] = acc / l


def _decode(page_table, q_lat, q_pe, ckv_own, kpe_own, cache_kv, cache_rope_t, *, sm_scale):
    bs, n_heads, kv_lora = q_lat.shape
    n_pages = page_table.shape[1]
    page = cache_kv.shape[1]
    assert n_pages % DECODE_CHUNK_PAGES == 0
    whole = lambda a: pl.BlockSpec(a.shape, lambda b, pt, nd=a.ndim: (0,) * nd)
    grid_spec = pltpu.PrefetchScalarGridSpec(
        num_scalar_prefetch=1,
        grid=(bs,),
        in_specs=[whole(q_lat), whole(q_pe), whole(ckv_own), whole(kpe_own),
                  pl.BlockSpec(memory_space=pl.ANY), pl.BlockSpec(memory_space=pl.ANY)],
        out_specs=pl.BlockSpec((bs, n_heads, kv_lora), lambda b, pt: (0, 0, 0)),
        scratch_shapes=[
            pltpu.VMEM((2, n_pages, page, kv_lora), F32),
            pltpu.VMEM((2, n_pages, QK_ROPE, page), F32),
            pltpu.SemaphoreType.DMA((2,)),
            pltpu.SemaphoreType.DMA((2,)),
        ],
    )
    return pl.pallas_call(
        functools.partial(_decode_kernel, n_pages=n_pages, sm_scale=sm_scale),
        out_shape=jax.ShapeDtypeStruct((bs, n_heads, kv_lora), F32),
        grid_spec=grid_spec,
        compiler_params=_params(1),
        name="decode",
    )(page_table.reshape(-1), q_lat, q_pe, ckv_own, kpe_own, cache_kv, cache_rope_t)


def _sample_kernel(olat_ref, wuv_ref, sp_ref, us_ref, wp_ref, scale_ref, pool_ref, attn_ref,
                   *, n_heads, group, past_len):
    for h in range(n_heads):
        attn_ref[:, h * V_HEAD:(h + 1) * V_HEAD] = _dot(olat_ref[h].astype(BF16), wuv_ref[h]).astype(BF16)
    for g, w in enumerate(POOL_WINDOWS):
        cols = slice(g * group, (g + 1) * group)
        cur = us_ref[:, cols]
        acc = cur
        for k in range(1, w):
            acc = acc + sp_ref[POOL_STATE - k, :, cols]
        delta = _pool_delta(acc, cur, 1.0 / min(past_len + 1, w))
        out = _dot(delta.astype(BF16), wp_ref[g])
        pool_ref[:, cols] = (out * scale_ref[:, cols]).astype(BF16)


def _sample_mix(o_lat, w_uv_h, state_pool, u_s, w_pool, pool_scale, *, past_len):
    n_heads, ms, _ = o_lat.shape
    d_pool = u_s.shape[1]
    return pl.pallas_call(
        functools.partial(_sample_kernel, n_heads=n_heads, group=d_pool // len(POOL_WINDOWS),
                          past_len=past_len),
        out_shape=(jax.ShapeDtypeStruct((ms, d_pool), BF16),
                   jax.ShapeDtypeStruct((ms, n_heads * V_HEAD), BF16)),
        compiler_params=pltpu.CompilerParams(vmem_limit_bytes=VMEM_LIMIT),
        name="sample_mix",
    )(o_lat, w_uv_h, state_pool, u_s, w_pool, pool_scale)


def _outproj_kernel(hp_ref, poolp_ref, attnp_ref, hs_ref, pools_ref, attns_ref, wo_ref, g_ref,
                    yp_ref, ys_ref, *, n_prompt_tiles, d_pool):
    i = pl.program_id(0)

    def body(h_ref, pool_ref, attn, y_ref):
        mix = _dot(pool_ref[...], wo_ref[:d_pool, :]) + _dot(attn, wo_ref[d_pool:, :])
        y_ref[...] = h_ref[...] + _rms(mix, g_ref[...])

    @pl.when(i < n_prompt_tiles)
    def _():
        heads = [attnp_ref[h] for h in range(attnp_ref.shape[0])]
        body(hp_ref, poolp_ref, jnp.concatenate(heads, axis=1), yp_ref)

    @pl.when(i == n_prompt_tiles)
    def _():
        body(hs_ref, pools_ref, attns_ref[...], ys_ref)


def _outproj(hp, pool_p, attn_p, hs, pool_s, attn_s, w_out, g_post):
    mp, d = hp.shape
    ms = hs.shape[0]
    d_pool = pool_p.shape[1]
    n_heads, _, v_head = attn_p.shape
    tm = ROW_TILE
    ntp = mp // tm
    prompt_row = lambda i: (jnp.minimum(i, ntp - 1), 0)
    fixed = lambda i: (0, 0)
    return pl.pallas_call(
        functools.partial(_outproj_kernel, n_prompt_tiles=ntp, d_pool=d_pool),
        out_shape=(jax.ShapeDtypeStruct((mp, d), F32), jax.ShapeDtypeStruct((ms, d), F32)),
        grid=(ntp + 1,),
        in_specs=[
            pl.BlockSpec((tm, d), prompt_row),
            pl.BlockSpec((tm, d_pool), prompt_row),
            pl.BlockSpec((n_heads, tm, v_head), lambda i: (0, jnp.minimum(i, ntp - 1), 0)),
            pl.BlockSpec((ms, d), fixed),
            pl.BlockSpec((ms, d_pool), fixed),
            pl.BlockSpec((ms, n_heads * v_head), fixed),
            pl.BlockSpec(w_out.shape, fixed),
            pl.BlockSpec((1, d), fixed),
        ],
        out_specs=(pl.BlockSpec((tm, d), prompt_row), pl.BlockSpec((ms, d), fixed)),
        compiler_params=_params(1),
        name="outproj",
    )(hp, pool_p, attn_p, hs, pool_s, attn_s, w_out, g_post)


def _rotate_half_columns(w):
    half = QK_ROPE // 2
    return jnp.concatenate([-w[..., half:], w[..., :half]], axis=-1)


def _rope_table(pos):
    half = QK_ROPE // 2
    inv = ROPE_THETA ** (-jnp.arange(half, dtype=F32) / half)
    ang = pos.astype(F32)[:, None] * inv[None, :]
    cos, sin = jnp.cos(ang), jnp.sin(ang)
    return jnp.concatenate([cos, cos, sin, sin], axis=-1)


def kernel(x_prompt, x_sample, cache_kv_latent, cache_k_rope, state_pool, page_table,
           g_ffn1_pre, w1_gate, w1_up, w1_down, g_ffn1_post,
           g_mix_pre, w_in, w_pool, pool_scale, g_q, w_uq, g_kv, w_uk, w_uv, w_out, g_mix_post,
           g_ffn2_pre, w2_gate, w2_up, w2_down, g_ffn2_post):
    depth = w_in.shape[0]
    bp, sp, d = x_prompt.shape
    bs, ts, _ = x_sample.shape
    n_pages = page_table.shape[1]
    page = cache_kv_latent.shape[2]
    past_len = n_pages * page
    kv_lora, n_heads, _ = w_uk.shape[1:]
    q_lora = w_uq.shape[1]
    d_pool = w_pool.shape[1] * w_pool.shape[2]
    assert ts == 1, "the decode kernel handles one new token per sample sequence"
    sm_scale = (QK_NOPE + QK_ROPE) ** -0.5

    cs_p = _rope_table(jnp.arange(sp, dtype=jnp.int32))
    cs_s = jnp.tile(_rope_table(past_len + jnp.arange(ts, dtype=jnp.int32)), (bs, 1))

    hp = x_prompt.reshape(bp * sp, d)
    hs = x_sample.reshape(bs * ts, d)
    outs = [[] for _ in range(6)]
    for l in range(depth):
        row = lambda g: g[l][None, :]
        w_in_t = jnp.swapaxes(w_in[l], 0, 1)
        w_rope_t = jnp.swapaxes(_rotate_half_columns(w_in[l][:, d_pool + q_lora + kv_lora:]), 0, 1)
        w_in2 = jnp.concatenate([w_in_t, w_rope_t], axis=0).astype(BF16)
        wq = w_uq[l].reshape(q_lora, n_heads, QK_NOPE + QK_ROPE)
        w_q2 = jnp.concatenate(
            [wq, _rotate_half_columns(wq[..., QK_NOPE:])], axis=-1).reshape(q_lora, n_heads * HEAD_W).astype(BF16)
        w_kv2 = jnp.concatenate(
            [w_uk[l].reshape(kv_lora, -1), w_uv[l].reshape(kv_lora, -1)], axis=1).astype(BF16)
        w_uk_t = jnp.transpose(w_uk[l], (1, 2, 0)).astype(BF16)
        w_uv_h = jnp.transpose(w_uv[l], (1, 0, 2)).astype(BF16)
        w_pool_l = w_pool[l].astype(BF16)
        w_out_l = w_out[l].astype(BF16)

        hp, hs = _ffn(hp, hs, row(g_ffn1_pre), w1_gate[l], w1_up[l], w1_down[l], row(g_ffn1_post))
        (u_p, ckv_p, kpe_p, q_p, k_p, v_p, u_s, ckv_s, kpe_s, qlat_s, qpe_s) = _proj(
            hp, hs, cs_p, cs_s, row(g_mix_pre), w_in2, row(g_q), w_q2, row(g_kv), w_kv2, w_uk_t,
            seq=sp, n_heads=n_heads)
        pool_p = _pool(u_p, w_pool_l, row(pool_scale), batch=bp, seq=sp)
        attn_p = _attn(q_p, k_p, v_p, batch=bp, seq=sp, sm_scale=sm_scale)
        o_lat = _decode(page_table, jnp.transpose(qlat_s, (1, 0, 2)), jnp.transpose(qpe_s, (1, 0, 2)),
                        ckv_s, kpe_s, cache_kv_latent[l],
                        jnp.swapaxes(cache_k_rope[l], 1, 2),
                        sm_scale=sm_scale)
        state_t = jnp.transpose(state_pool[l], (1, 0, 2))
        pool_s, attn_s = _sample_mix(jnp.transpose(o_lat, (1, 0, 2)), w_uv_h, state_t, u_s,
                                     w_pool_l, row(pool_scale), past_len=past_len)
        hp, hs = _outproj(hp, pool_p, attn_p, hs, pool_s, attn_s, w_out_l, row(g_mix_post))
        hp, hs = _ffn(hp, hs, row(g_ffn2_pre), w2_gate[l], w2_up[l], w2_down[l], row(g_ffn2_post))

        u_p3 = u_p.reshape(bp, sp, d_pool)
        outs[0].append(ckv_p.reshape(bp, sp, kv_lora))
        outs[1].append(kpe_p.reshape(bp, sp, QK_ROPE))
        outs[2].append(u_p3[:, sp - POOL_STATE:])
        outs[3].append(ckv_s.reshape(bs, ts, kv_lora))
        outs[4].append(kpe_s.reshape(bs, ts, QK_ROPE))
        new_rows_t = jnp.transpose(u_s.reshape(bs, ts, d_pool), (1, 0, 2))
        outs[5].append(jnp.transpose(jnp.concatenate([state_t[ts:], new_rows_t], axis=0), (1, 0, 2)))
    return (hp.reshape(bp, sp, d), hs.reshape(bs, ts, d), *[jnp.stack(o) for o in outs])
```

```python
import functools
import math

import jax
import jax.numpy as jnp
from jax import lax
from jax.experimental import pallas as pl
from jax.experimental.pallas import tpu as pltpu

F32 = jnp.float32
BF16 = jnp.bfloat16

EPS = 1e-6
ROPE_THETA = 10000.0
POOL_WINDOWS = (2, 4, 8, 16)
POOL_STATE = max(POOL_WINDOWS) - 1
QK_NOPE = 128
QK_ROPE = 64
V_HEAD = 128
HEAD_W = 2 * 128

V7X_VMEM_BYTES = 64 * 1024 * 1024
VMEM_LIMIT = V7X_VMEM_BYTES - 8 * 1024 * 1024

ROW_TILE = 512
FFN_ROW_TILE = 1024
FF_TILE = 256
NORM_ROWS = 128
PROJ_TILE = 512
POOL_TILE = 512
ATTN_TILE = 512
ATTN_HEADS_IN_FLIGHT = 4
DECODE_CHUNK_PAGES = 32


def _params(n_grid_dims):
    return pltpu.CompilerParams(
        dimension_semantics=("arbitrary",) * n_grid_dims, vmem_limit_bytes=VMEM_LIMIT)


def _rms(x, g):
    return x * lax.rsqrt(jnp.mean(x * x, axis=-1, keepdims=True) + EPS) * g


def _dot(a, b):
    return jnp.dot(a, b, preferred_element_type=F32)


def _dot_nt(a, b):
    return lax.dot_general(a, b, (((1,), (1,)), ((), ())), preferred_element_type=F32)


def _for_row_chunks(rows, fn):
    n_chunks, rem = divmod(rows, NORM_ROWS)
    assert rem == 0

    def step(c, carry):
        fn(pl.ds(pl.multiple_of(c * NORM_ROWS, NORM_ROWS), NORM_ROWS))
        return carry

    lax.fori_loop(0, n_chunks, step, 0, unroll=2 if n_chunks % 2 == 0 else 1)


def _row_inv_rms(x):
    return lax.rsqrt(jnp.mean(x * x, axis=-1, keepdims=True) + EPS)


def _ffn_kernel(xp_ref, xs_ref, gpre_ref, wg_hbm, wu_hbm, wd_hbm, gpost_ref,
                yp_ref, ys_ref, xnp_ref, xns_ref, wg_buf, wu_buf, wd_buf, w_sem, *, n_ff_tiles):
    row_tile = pl.program_id(0)
    last_tile = row_tile == pl.num_programs(0) - 1
    tm = xp_ref.shape[0]
    n_pairs = n_ff_tiles // 2

    def weight_copies(j, slot):
        cols = pl.ds(pl.multiple_of(j * FF_TILE, FF_TILE), FF_TILE)
        return (pltpu.make_async_copy(wg_hbm.at[:, cols], wg_buf.at[slot], w_sem.at[0, slot]),
                pltpu.make_async_copy(wu_hbm.at[:, cols], wu_buf.at[slot], w_sem.at[1, slot]),
                pltpu.make_async_copy(wd_hbm.at[cols, :], wd_buf.at[slot], w_sem.at[2, slot]))

    def start(j, slot):
        for c in weight_copies(j, slot):
            c.start()

    def wait(j, slot):
        for c in weight_copies(j, slot):
            c.wait()

    def prologue(x_ref, y_ref, xn_ref):
        def chunk(rs):
            inv = _row_inv_rms(x_ref[rs, :])
            xn_ref[rs, :] = (x_ref[rs, :] * inv * gpre_ref[...]).astype(BF16)
            y_ref[rs, :] = jnp.zeros((NORM_ROWS, y_ref.shape[1]), F32)

        _for_row_chunks(x_ref.shape[0], chunk)

    def epilogue(x_ref, y_ref):
        half_gain = 0.5 * gpost_ref[...]

        def chunk(rs):
            inv = _row_inv_rms(y_ref[rs, :])
            y_ref[rs, :] = x_ref[rs, :] + y_ref[rs, :] * inv * half_gain

        _for_row_chunks(x_ref.shape[0], chunk)

    def swiglu_part(xn, slot):
        gate = _dot(xn, wg_buf[slot].astype(BF16))
        up = _dot(xn, wu_buf[slot].astype(BF16))
        hidden = (gate / (1.0 + jnp.exp(-gate))) * up
        return _dot(hidden.astype(BF16), wd_buf[slot].astype(BF16))

    def accumulate(slot, with_sample):
        if with_sample:
            part = swiglu_part(jnp.concatenate([xnp_ref[...], xns_ref[...]], axis=0), slot)
            yp_ref[...] += part[:tm]
            ys_ref[...] += part[tm:]
        else:
            yp_ref[...] += swiglu_part(xnp_ref[...], slot)

    def ff_loop(with_sample):
        def pair(jj, carry):
            j = 2 * jj
            start(j + 1, 1)
            wait(j, 0)
            accumulate(0, with_sample)
            if with_sample:
                pl.when(jj < n_pairs - 1)(lambda: start(j + 2, 0))
            else:
                start(lax.rem(j + 2, n_ff_tiles), 0)
            wait(j + 1, 1)
            accumulate(1, with_sample)
            return carry

        lax.fori_loop(0, n_pairs, pair, 0)

    pl.when(row_tile == 0)(lambda: start(0, 0))
    prologue(xp_ref, yp_ref, xnp_ref)
    pl.when(last_tile)(lambda: prologue(xs_ref, ys_ref, xns_ref))
    pl.when(jnp.logical_not(last_tile))(lambda: ff_loop(False))
    pl.when(last_tile)(lambda: ff_loop(True))
    epilogue(xp_ref, yp_ref)
    pl.when(last_tile)(lambda: epilogue(xs_ref, ys_ref))


def _ffn(xp, xs, g_pre, w_gate, w_up, w_down, g_post):
    mp, d = xp.shape
    ms = xs.shape[0]
    d_ff = w_gate.shape[1]
    tm = FFN_ROW_TILE
    nj = d_ff // FF_TILE
    assert mp % tm == 0 and d_ff % FF_TILE == 0 and nj % 2 == 0
    prompt_row = lambda i: (i, 0)
    fixed = lambda i: (0, 0)
    in_hbm = pl.BlockSpec(memory_space=pl.ANY)
    return pl.pallas_call(
        functools.partial(_ffn_kernel, n_ff_tiles=nj),
        out_shape=(jax.ShapeDtypeStruct((mp, d), F32), jax.ShapeDtypeStruct((ms, d), F32)),
        grid=(mp // tm,),
        in_specs=[
            pl.BlockSpec((tm, d), prompt_row),
            pl.BlockSpec((ms, d), fixed),
            pl.BlockSpec((1, d), fixed),
            in_hbm, in_hbm, in_hbm,
            pl.BlockSpec((1, d), fixed),
        ],
        out_specs=(pl.BlockSpec((tm, d), prompt_row), pl.BlockSpec((ms, d), fixed)),
        scratch_shapes=[
            pltpu.VMEM((tm, d), BF16), pltpu.VMEM((ms, d), BF16),
            pltpu.VMEM((2, d, FF_TILE), F32), pltpu.VMEM((2, d, FF_TILE), F32),
            pltpu.VMEM((2, FF_TILE, d), F32),
            pltpu.SemaphoreType.DMA((3, 2)),
        ],
        compiler_params=_params(1),
        name="ffn",
    )(xp, xs, g_pre, w_gate, w_up, w_down, g_post)


def _rope_pairs(slab, cs):
    t = slab * cs
    return t + pltpu.roll(t, QK_ROPE, axis=1)


def _proj_kernel(hp_ref, hs_ref, csp_ref, css_ref, gmix_ref, win_ref, gq_ref, wq_ref, gkv_ref,
                 wkv_ref, wukt_ref,
                 up_ref, ckvp_ref, kpep_ref, q_ref, k_ref, v_ref,
                 us_ref, ckvs_ref, kpes_ref, qlat_ref, qpes_ref,
                 *, n_prompt_tiles, n_heads, d_pool, q_lora, kv_lora):
    i = pl.program_id(0)

    def common(h_ref, cs_ref, u_ref, ckv_ref, kpe_ref):
        hn = _rms(h_ref[...], gmix_ref[...]).astype(BF16)
        z = _dot_nt(hn, win_ref[...])
        u_ref[...] = z[:, :d_pool]
        cqn = _rms(z[:, d_pool:d_pool + q_lora], gq_ref[...]).astype(BF16)
        ckv = _rms(z[:, d_pool + q_lora:d_pool + q_lora + kv_lora], gkv_ref[...])
        ckv_ref[...] = ckv
        cs = cs_ref[...]
        kpe = _rope_pairs(z[:, d_pool + q_lora + kv_lora:], cs)
        kpe_ref[...] = kpe[:, :QK_ROPE]
        qq = _dot(cqn, wq_ref[...])
        return ckv, kpe, qq, cs

    @pl.when(i < n_prompt_tiles)
    def _():
        ckv, kpe, qq, cs = common(hp_ref, csp_ref, up_ref, ckvp_ref, kpep_ref)
        lane = lax.broadcasted_iota(jnp.int32, kpe.shape, 1)
        kpe_pad = jnp.where(lane < QK_ROPE, kpe, 0.0).astype(BF16)
        kv = _dot(ckv.astype(BF16), wkv_ref[...])
        for h in range(n_heads):
            v_lo = n_heads * QK_NOPE + h * V_HEAD
            v_ref[h] = kv[:, v_lo:v_lo + V_HEAD].astype(BF16)
            slab = qq[:, h * HEAD_W:(h + 1) * HEAD_W]
            q_ref[h, :, :QK_NOPE] = slab[:, :QK_NOPE].astype(BF16)
            q_ref[h, :, QK_NOPE:] = _rope_pairs(slab[:, QK_NOPE:], cs).astype(BF16)
            k_ref[h, :, :QK_NOPE] = kv[:, h * QK_NOPE:(h + 1) * QK_NOPE].astype(BF16)
            k_ref[h, :, QK_NOPE:] = kpe_pad

    @pl.when(i == n_prompt_tiles)
    def _():
        _, _, qq, cs = common(hs_ref, css_ref, us_ref, ckvs_ref, kpes_ref)
        for h in range(n_heads):
            slab = qq[:, h * HEAD_W:(h + 1) * HEAD_W]
            qlat_ref[h] = _dot(slab[:, :QK_NOPE].astype(BF16), wukt_ref[h]).astype(BF16)
            qpes_ref[h] = _rope_pairs(slab[:, QK_NOPE:], cs)[:, :QK_ROPE].astype(BF16)


def _proj(hp, hs, cs_p, cs_s, g_mix, w_in2, g_q, w_q2, g_kv, w_kv2, w_uk_t, *, seq, n_heads):
    mp, d = hp.shape
    ms = hs.shape[0]
    tm = PROJ_TILE
    ntp = mp // tm
    tiles_per_seq = seq // tm
    assert mp % tm == 0 and seq % tm == 0
    d_in2 = w_in2.shape[0]
    q_lora = w_q2.shape[0]
    kv_lora = w_kv2.shape[0]
    d_pool = d_in2 - q_lora - kv_lora - 2 * QK_ROPE
    prompt_row = lambda i: (jnp.minimum(i, ntp - 1), 0)
    prompt_head_row = lambda i: (0, jnp.minimum(i, ntp - 1), 0)
    fixed2 = lambda i: (0, 0)
    fixed3 = lambda i: (0, 0, 0)
    whole = lambda a: pl.BlockSpec(a.shape, fixed2 if a.ndim == 2 else fixed3)
    out_shape = (
        jax.ShapeDtypeStruct((mp, d_pool), F32),
        jax.ShapeDtypeStruct((mp, kv_lora), F32),
        jax.ShapeDtypeStruct((mp, QK_ROPE), F32),
        jax.ShapeDtypeStruct((n_heads, mp, HEAD_W), BF16),
        jax.ShapeDtypeStruct((n_heads, mp, HEAD_W), BF16),
        jax.ShapeDtypeStruct((n_heads, mp, V_HEAD), BF16),
        jax.ShapeDtypeStruct((ms, d_pool), F32),
        jax.ShapeDtypeStruct((ms, kv_lora), F32),
        jax.ShapeDtypeStruct((ms, QK_ROPE), F32),
        jax.ShapeDtypeStruct((n_heads, ms, kv_lora), BF16),
        jax.ShapeDtypeStruct((n_heads, ms, QK_ROPE), BF16),
    )
    out_specs = (
        pl.BlockSpec((tm, d_pool), prompt_row),
        pl.BlockSpec((tm, kv_lora), prompt_row),
        pl.BlockSpec((tm, QK_ROPE), prompt_row),
        pl.BlockSpec((n_heads, tm, HEAD_W), prompt_head_row),
        pl.BlockSpec((n_heads, tm, HEAD_W), prompt_head_row),
        pl.BlockSpec((n_heads, tm, V_HEAD), prompt_head_row),
        pl.BlockSpec((ms, d_pool), fixed2),
        pl.BlockSpec((ms, kv_lora), fixed2),
        pl.BlockSpec((ms, QK_ROPE), fixed2),
        pl.BlockSpec((n_heads, ms, kv_lora), fixed3),
        pl.BlockSpec((n_heads, ms, QK_ROPE), fixed3),
    )
    return pl.pallas_call(
        functools.partial(_proj_kernel, n_prompt_tiles=ntp, n_heads=n_heads, d_pool=d_pool,
                          q_lora=q_lora, kv_lora=kv_lora),
        out_shape=out_shape,
        grid=(ntp + 1,),
        in_specs=[
            pl.BlockSpec((tm, d), prompt_row),
            pl.BlockSpec((ms, d), fixed2),
            pl.BlockSpec((tm, 2 * QK_ROPE), lambda i: (jnp.minimum(i, ntp - 1) % tiles_per_seq, 0)),
            whole(cs_s), whole(g_mix), whole(w_in2), whole(g_q), whole(w_q2), whole(g_kv),
            whole(w_kv2), whole(w_uk_t),
        ],
        out_specs=out_specs,
        compiler_params=_params(1),
        name="proj",
    )(hp, hs, cs_p, cs_s, g_mix, w_in2, g_q, w_q2, g_kv, w_kv2, w_uk_t)


def _pool_delta(window_sum, cur, inv_cnt):
    return window_sum * inv_cnt - cur


def _pool_kernel(u_ref, wp_ref, scale_ref, o_ref, ext_ref, *, group):
    i = pl.program_id(1)
    ts = POOL_TILE
    halo = POOL_STATE + 1

    @pl.when(i == 0)
    def _():
        ext_ref[:halo, :] = jnp.zeros((halo, ext_ref.shape[1]), F32)

    @pl.when(i > 0)
    def _():
        ext_ref[:halo, :] = ext_ref[ts:ts + halo, :]

    ext_ref[halo:, :] = u_ref[...]
    pos = i * ts + lax.broadcasted_iota(jnp.int32, (ts, 1), 0)
    for g, w in enumerate(POOL_WINDOWS):
        cols = slice(g * group, (g + 1) * group)
        cur = ext_ref[halo:, cols]
        acc = cur
        for k in range(1, w):
            acc = acc + ext_ref[halo - k:halo - k + ts, cols]
        inv_cnt = 1.0 / jnp.minimum(pos + 1, w).astype(F32)
        delta = _pool_delta(acc, cur, inv_cnt)
        out = _dot(delta.astype(BF16), wp_ref[g])
        o_ref[:, cols] = (out * scale_ref[:, cols]).astype(BF16)


def _pool(u, w_pool, pool_scale, *, batch, seq):
    m, d_pool = u.shape
    ts = POOL_TILE
    nt = seq // ts
    group = d_pool // len(POOL_WINDOWS)
    assert seq % ts == 0 and m == batch * seq
    return pl.pallas_call(
        functools.partial(_pool_kernel, group=group),
        out_shape=jax.ShapeDtypeStruct((m, d_pool), BF16),
        grid=(batch, nt),
        in_specs=[
            pl.BlockSpec((ts, d_pool), lambda b, i: (b * nt + i, 0)),
            pl.BlockSpec(w_pool.shape, lambda b, i: (0, 0, 0)),
            pl.BlockSpec((1, d_pool), lambda b, i: (0, 0)),
        ],
        out_specs=pl.BlockSpec((ts, d_pool), lambda b, i: (b * nt + i, 0)),
        scratch_shapes=[pltpu.VMEM((ts + POOL_STATE + 1, d_pool), F32)],
        compiler_params=_params(2),
        name="pool",
    )(u, w_pool, pool_scale)


def _attn_kernel(q_ref, k_ref, v_ref, o_ref, *, n_heads, n_q_tiles, exp2_scale):
    qi = pl.program_id(1)
    t = ATTN_TILE
    row = lax.broadcasted_iota(jnp.int32, (t, t), 0)
    col = lax.broadcasted_iota(jnp.int32, (t, t), 1)
    causal = row >= col

    def tile(n_past):
        def head(h, carry):
            q = q_ref[h]
            s_diag = jnp.where(causal, _dot_nt(q, k_ref[h, n_past:n_past + t, :]), -jnp.inf)
            m = jnp.max(s_diag, axis=1, keepdims=True)
            if n_past:
                s_past = _dot_nt(q, k_ref[h, :n_past, :])
                m = jnp.maximum(m, jnp.max(s_past, axis=1, keepdims=True))
            p_diag = jnp.exp2((s_diag - m) * exp2_scale)
            l = jnp.sum(p_diag, axis=1, keepdims=True)
            o = _dot(p_diag.astype(BF16), v_ref[h, n_past:n_past + t, :])
            if n_past:
                p_past = jnp.exp2((s_past - m) * exp2_scale)
                l = l + jnp.sum(p_past, axis=1, keepdims=True)
                o = o + _dot(p_past.astype(BF16), v_ref[h, :n_past, :])
            o_ref[h] = (o / l).astype(BF16)
            return carry

        lax.fori_loop(0, n_heads, head, 0, unroll=ATTN_HEADS_IN_FLIGHT)

    for kq in range(n_q_tiles):
        pl.when(qi == kq)(functools.partial(tile, kq * t))


def _attn(q, k, v, *, batch, seq, sm_scale):
    n_heads, m, _ = q.shape
    t = ATTN_TILE
    nq = seq // t
    assert seq % t == 0
    return pl.pallas_call(
        functools.partial(_attn_kernel, n_heads=n_heads, n_q_tiles=nq,
                          exp2_scale=sm_scale * math.log2(math.e)),
        out_shape=jax.ShapeDtypeStruct((n_heads, m, V_HEAD), BF16),
        grid=(batch, nq),
        in_specs=[
            pl.BlockSpec((n_heads, t, HEAD_W), lambda b, i: (0, b * nq + i, 0)),
            pl.BlockSpec((n_heads, seq, HEAD_W), lambda b, i: (0, b, 0)),
            pl.BlockSpec((n_heads, seq, V_HEAD), lambda b, i: (0, b, 0)),
        ],
        out_specs=pl.BlockSpec((n_heads, t, V_HEAD), lambda b, i: (0, b * nq + i, 0)),
        compiler_params=_params(2),
        name="attn",
    )(q, k, v)


def _decode_kernel(pt_ref, qlat_ref, qpe_ref, ckvs_ref, kpes_ref, kv_hbm, rope_hbm, o_ref,
                   kv_buf, rope_buf, kv_sem, rope_sem, *, n_pages, sm_scale):
    b = pl.program_id(0)
    n_samples = pl.num_programs(0)
    slot = b % 2

    def page_copies(sample, sl):
        copies = []
        for i in range(n_pages):
            page_id = pt_ref[sample * n_pages + i]
            copies.append(pltpu.make_async_copy(kv_hbm.at[page_id], kv_buf.at[sl, i], kv_sem.at[sl]))
            copies.append(pltpu.make_async_copy(rope_hbm.at[page_id], rope_buf.at[sl, i], rope_sem.at[sl]))
        return copies

    @pl.when(b == 0)
    def _():
        for c in page_copies(0, 0):
            c.start()

    @pl.when(b + 1 < n_samples)
    def _():
        for c in page_copies(b + 1, 1 - slot):
            c.start()

    for c in page_copies(b, slot):
        c.wait()

    qlat = qlat_ref[b]
    qpe = qpe_ref[b]
    own_ckv = ckvs_ref[pl.ds(b, 1), :].astype(BF16).astype(F32)
    own_kpe = kpes_ref[pl.ds(b, 1), :].astype(BF16).astype(F32)
    s_own = (jnp.sum(qlat.astype(F32) * own_ckv, axis=1, keepdims=True)
             + jnp.sum(qpe.astype(F32) * own_kpe, axis=1, keepdims=True)) * sm_scale

    chunk = DECODE_CHUNK_PAGES
    ckv_parts, s_parts = [], []
    for c0 in range(0, n_pages, chunk):
        ckv = jnp.concatenate([kv_buf[slot, i] for i in range(c0, c0 + chunk)], axis=0).astype(BF16)
        kpe_t = jnp.concatenate([rope_buf[slot, i] for i in range(c0, c0 + chunk)], axis=1).astype(BF16)
        ckv_parts.append(ckv)
        s_parts.append((_dot_nt(qlat, ckv) + _dot(qpe, kpe_t)) * sm_scale)
    m = s_own
    for s in s_parts:
        m = jnp.maximum(m, jnp.max(s, axis=1, keepdims=True))
    p_own = jnp.exp(s_own - m)
    l = p_own
    acc = p_own.astype(BF16).astype(F32) * own_ckv
    for s, ckv in zip(s_parts, ckv_parts):
        p = jnp.exp(s - m)
        l = l + jnp.sum(p, axis=1, keepdims=True)
        acc = acc + _dot(p.astype(BF16), ckv)
    o_ref[b] = acc / l


def _decode(page_table, q_lat, q_pe, ckv_own, kpe_own, cache_kv, cache_rope_t, *, sm_scale):
    bs, n_heads, kv_lora = q_lat.shape
    n_pages = page_table.shape[1]
    page = cache_kv.shape[1]
    assert n_pages % DECODE_CHUNK_PAGES == 0
    whole = lambda a: pl.BlockSpec(a.shape, lambda b, pt, nd=a.ndim: (0,) * nd)
    grid_spec = pltpu.PrefetchScalarGridSpec(
        num_scalar_prefetch=1,
        grid=(bs,),
        in_specs=[whole(q_lat), whole(q_pe), whole(ckv_own), whole(kpe_own),
                  pl.BlockSpec(memory_space=pl.ANY), pl.BlockSpec(memory_space=pl.ANY)],
        out_specs=pl.BlockSpec((bs, n_heads, kv_lora), lambda b, pt: (0, 0, 0)),
        scratch_shapes=[
            pltpu.VMEM((2, n_pages, page, kv_lora), F32),
            pltpu.VMEM((2, n_pages, QK_ROPE, page), F32),
            pltpu.SemaphoreType.DMA((2,)),
            pltpu.SemaphoreType.DMA((2,)),
        ],
    )
    return pl.pallas_call(
        functools.partial(_decode_kernel, n_pages=n_pages, sm_scale=sm_scale),
        out_shape=jax.ShapeDtypeStruct((bs, n_heads, kv_lora), F32),
        grid_spec=grid_spec,
        compiler_params=_params(1),
        name="decode",
    )(page_table.reshape(-1), q_lat, q_pe, ckv_own, kpe_own, cache_kv, cache_rope_t)


def _sample_kernel(olat_ref, wuv_ref, sp_ref, us_ref, wp_ref, scale_ref, pool_ref, attn_ref,
                   *, n_heads, group, past_len):
    for h in range(n_heads):
        attn_ref[:, h * V_HEAD:(h + 1) * V_HEAD] = _dot(olat_ref[h].astype(BF16), wuv_ref[h]).astype(BF16)
    for g, w in enumerate(POOL_WINDOWS):
        cols = slice(g * group, (g + 1) * group)
        cur = us_ref[:, cols]
        acc = cur
        for k in range(1, w):
            acc = acc + sp_ref[POOL_STATE - k, :, cols]
        delta = _pool_delta(acc, cur, 1.0 / min(past_len + 1, w))
        out = _dot(delta.astype(BF16), wp_ref[g])
        pool_ref[:, cols] = (out * scale_ref[:, cols]).astype(BF16)


def _sample_mix(o_lat, w_uv_h, state_pool, u_s, w_pool, pool_scale, *, past_len):
    n_heads, ms, _ = o_lat.shape
    d_pool = u_s.shape[1]
    return pl.pallas_call(
        functools.partial(_sample_kernel, n_heads=n_heads, group=d_pool // len(POOL_WINDOWS),
                          past_len=past_len),
        out_shape=(jax.ShapeDtypeStruct((ms, d_pool), BF16),
                   jax.ShapeDtypeStruct((ms, n_heads * V_HEAD), BF16)),
        compiler_params=pltpu.CompilerParams(vmem_limit_bytes=VMEM_LIMIT),
        name="sample_mix",
    )(o_lat, w_uv_h, state_pool, u_s, w_pool, pool_scale)


def _outproj_kernel(hp_ref, poolp_ref, attnp_ref, hs_ref, pools_ref, attns_ref, wo_ref, g_ref,
                    yp_ref, ys_ref, *, n_prompt_tiles, d_pool):
    i = pl.program_id(0)

    def body(h_ref, pool_ref, attn, y_ref):
        mix = _dot(pool_ref[...], wo_ref[:d_pool, :]) + _dot(attn, wo_ref[d_pool:, :])
        y_ref[...] = h_ref[...] + _rms(mix, g_ref[...])

    @pl.when(i < n_prompt_tiles)
    def _():
        heads = [attnp_ref[h] for h in range(attnp_ref.shape[0])]
        body(hp_ref, poolp_ref, jnp.concatenate(heads, axis=1), yp_ref)

    @pl.when(i == n_prompt_tiles)
    def _():
        body(hs_ref, pools_ref, attns_ref[...], ys_ref)


def _outproj(hp, pool_p, attn_p, hs, pool_s, attn_s, w_out, g_post):
    mp, d = hp.shape
    ms = hs.shape[0]
    d_pool = pool_p.shape[1]
    n_heads, _, v_head = attn_p.shape
    tm = ROW_TILE
    ntp = mp // tm
    prompt_row = lambda i: (jnp.minimum(i, ntp - 1), 0)
    fixed = lambda i: (0, 0)
    return pl.pallas_call(
        functools.partial(_outproj_kernel, n_prompt_tiles=ntp, d_pool=d_pool),
        out_shape=(jax.ShapeDtypeStruct((mp, d), F32), jax.ShapeDtypeStruct((ms, d), F32)),
        grid=(ntp + 1,),
        in_specs=[
            pl.BlockSpec((tm, d), prompt_row),
            pl.BlockSpec((tm, d_pool), prompt_row),
            pl.BlockSpec((n_heads, tm, v_head), lambda i: (0, jnp.minimum(i, ntp - 1), 0)),
            pl.BlockSpec((ms, d), fixed),
            pl.BlockSpec((ms, d_pool), fixed),
            pl.BlockSpec((ms, n_heads * v_head), fixed),
            pl.BlockSpec(w_out.shape, fixed),
            pl.BlockSpec((1, d), fixed),
        ],
        out_specs=(pl.BlockSpec((tm, d), prompt_row), pl.BlockSpec((ms, d), fixed)),
        compiler_params=_params(1),
        name="outproj",
    )(hp, pool_p, attn_p, hs, pool_s, attn_s, w_out, g_post)


def _rotate_half_columns(w):
    half = QK_ROPE // 2
    return jnp.concatenate([-w[..., half:], w[..., :half]], axis=-1)


def _rope_table(pos):
    half = QK_ROPE // 2
    inv = ROPE_THETA ** (-jnp.arange(half, dtype=F32) / half)
    ang = pos.astype(F32)[:, None] * inv[None, :]
    cos, sin = jnp.cos(ang), jnp.sin(ang)
    return jnp.concatenate([cos, cos, sin, sin], axis=-1)


def kernel(x_prompt, x_sample, cache_kv_latent, cache_k_rope, state_pool, page_table,
           g_ffn1_pre, w1_gate, w1_up, w1_down, g_ffn1_post,
           g_mix_pre, w_in, w_pool, pool_scale, g_q, w_uq, g_kv, w_uk, w_uv, w_out, g_mix_post,
           g_ffn2_pre, w2_gate, w2_up, w2_down, g_ffn2_post):
    depth = w_in.shape[0]
    bp, sp, d = x_prompt.shape
    bs, ts, _ = x_sample.shape
    n_pages = page_table.shape[1]
    page = cache_kv_latent.shape[2]
    past_len = n_pages * page
    kv_lora, n_heads, _ = w_uk.shape[1:]
    q_lora = w_uq.shape[1]
    d_pool = w_pool.shape[1] * w_pool.shape[2]
    assert ts == 1, "the decode kernel handles one new token per sample sequence"
    sm_scale = (QK_NOPE + QK_ROPE) ** -0.5

    cs_p = _rope_table(jnp.arange(sp, dtype=jnp.int32))
    cs_s = jnp.tile(_rope_table(past_len + jnp.arange(ts, dtype=jnp.int32)), (bs, 1))

    hp = x_prompt.reshape(bp * sp, d)
    hs = x_sample.reshape(bs * ts, d)
    outs = [[] for _ in range(6)]
    for l in range(depth):
        row = lambda g: g[l][None, :]
        w_in_t = jnp.swapaxes(w_in[l], 0, 1)
        w_rope_t = jnp.swapaxes(_rotate_half_columns(w_in[l][:, d_pool + q_lora + kv_lora:]), 0, 1)
        w_in2 = jnp.concatenate([w_in_t, w_rope_t], axis=0).astype(BF16)
        wq = w_uq[l].reshape(q_lora, n_heads, QK_NOPE + QK_ROPE)
        w_q2 = jnp.concatenate(
            [wq, _rotate_half_columns(wq[..., QK_NOPE:])], axis=-1).reshape(q_lora, n_heads * HEAD_W).astype(BF16)
        w_kv2 = jnp.concatenate(
            [w_uk[l].reshape(kv_lora, -1), w_uv[l].reshape(kv_lora, -1)], axis=1).astype(BF16)
        w_uk_t = jnp.transpose(w_uk[l], (1, 2, 0)).astype(BF16)
        w_uv_h = jnp.transpose(w_uv[l], (1, 0, 2)).astype(BF16)
        w_pool_l = w_pool[l].astype(BF16)
        w_out_l = w_out[l].astype(BF16)

        hp, hs = _ffn(hp, hs, row(g_ffn1_pre), w1_gate[l], w1_up[l], w1_down[l], row(g_ffn1_post))
        (u_p, ckv_p, kpe_p, q_p, k_p, v_p, u_s, ckv_s, kpe_s, qlat_s, qpe_s) = _proj(
            hp, hs, cs_p, cs_s, row(g_mix_pre), w_in2, row(g_q), w_q2, row(g_kv), w_kv2, w_uk_t,
            seq=sp, n_heads=n_heads)
        pool_p = _pool(u_p, w_pool_l, row(pool_scale), batch=bp, seq=sp)
        attn_p = _attn(q_p, k_p, v_p, batch=bp, seq=sp, sm_scale=sm_scale)
        o_lat = _decode(page_table, jnp.transpose(qlat_s, (1, 0, 2)), jnp.transpose(qpe_s, (1, 0, 2)),
                        ckv_s, kpe_s, cache_kv_latent[l],
                        jnp.swapaxes(cache_k_rope[l], 1, 2),
                        sm_scale=sm_scale)
        state_t = jnp.transpose(state_pool[l], (1, 0, 2))
        pool_s, attn_s = _sample_mix(jnp.transpose(o_lat, (1, 0, 2)), w_uv_h, state_t, u_s,
                                     w_pool_l, row(pool_scale), past_len=past_len)
        hp, hs = _outproj(hp, pool_p, attn_p, hs, pool_s, attn_s, w_out_l, row(g_mix_post))
        hp, hs = _ffn(hp, hs, row(g_ffn2_pre), w2_gate[l], w2_up[l], w2_down[l], row(g_ffn2_post))

        u_p3 = u_p.reshape(bp, sp, d_pool)
        outs[0].append(ckv_p.reshape(bp, sp, kv_lora))
        outs[1].append(kpe_p.reshape(bp, sp, QK_ROPE))
        outs[2].append(u_p3[:, sp - POOL_STATE:])
        outs[3].append(ckv_s.reshape(bs, ts, kv_lora))
        outs[4].append(kpe_s.reshape(bs, ts, QK_ROPE))
        new_rows_t = jnp.transpose(u_s.reshape(bs, ts, d_pool), (1, 0, 2))
        outs[5].append(jnp.transpose(jnp.concatenate([state_t[ts:], new_rows_t], axis=0), (1, 0, 2)))
    return (hp.reshape(bp, sp, d), hs.reshape(bs, ts, d), *[jnp.stack(o) for o in outs])
```

```python
import functools
import math

import jax
import jax.numpy as jnp
from jax import lax
from jax.experimental import pallas as pl
from jax.experimental.pallas import tpu as pltpu

F32 = jnp.float32
BF16 = jnp.bfloat16

EPS = 1e-6
ROPE_THETA = 10000.0
POOL_WINDOWS = (2, 4, 8, 16)
POOL_STATE = max(POOL_WINDOWS) - 1
QK_NOPE = 128
QK_ROPE = 64
V_HEAD = 128
HEAD_W = 2 * 128

V7X_VMEM_BYTES = 64 * 1024 * 1024
VMEM_LIMIT = V7X_VMEM_BYTES - 8 * 1024 * 1024

ROW_TILE = 512
FFN_ROW_TILE = 1024
FF_TILE = 256
NORM_ROWS = 128
PROJ_TILE = 512
ATTN_TILE = 512
ATTN_HEADS_IN_FLIGHT = 4
DECODE_CHUNK_PAGES = 32


def _params(n_grid_dims):
    return pltpu.CompilerParams(
        dimension_semantics=("arbitrary",) * n_grid_dims, vmem_limit_bytes=VMEM_LIMIT)


def _rms(x, g):
    return x * lax.rsqrt(jnp.mean(x * x, axis=-1, keepdims=True) + EPS) * g


def _dot(a, b):
    return jnp.dot(a, b, preferred_element_type=F32)


def _dot_nt(a, b):
    return lax.dot_general(a, b, (((1,), (1,)), ((), ())), preferred_element_type=F32)


def _for_row_chunks(rows, fn):
    n_chunks, rem = divmod(rows, NORM_ROWS)
    assert rem == 0

    def step(c, carry):
        fn(pl.ds(pl.multiple_of(c * NORM_ROWS, NORM_ROWS), NORM_ROWS))
        return carry

    lax.fori_loop(0, n_chunks, step, 0, unroll=2 if n_chunks % 2 == 0 else 1)


def _row_inv_rms(x):
    return lax.rsqrt(jnp.mean(x * x, axis=-1, keepdims=True) + EPS)


def _ffn_kernel(xp_ref, xs_ref, gpre_ref, wg_hbm, wu_hbm, wd_hbm, gpost_ref,
                yp_ref, ys_ref, xnp_ref, xns_ref, wg_buf, wu_buf, wd_buf, w_sem, *, n_ff_tiles):
    row_tile = pl.program_id(0)
    last_tile = row_tile == pl.num_programs(0) - 1
    tm = xp_ref.shape[0]
    n_pairs = n_ff_tiles // 2

    def weight_copies(j, slot):
        cols = pl.ds(pl.multiple_of(j * FF_TILE, FF_TILE), FF_TILE)
        return (pltpu.make_async_copy(wg_hbm.at[:, cols], wg_buf.at[slot], w_sem.at[0, slot]),
                pltpu.make_async_copy(wu_hbm.at[:, cols], wu_buf.at[slot], w_sem.at[1, slot]),
                pltpu.make_async_copy(wd_hbm.at[cols, :], wd_buf.at[slot], w_sem.at[2, slot]))

    def start(j, slot):
        for c in weight_copies(j, slot):
            c.start()

    def wait(j, slot):
        for c in weight_copies(j, slot):
            c.wait()

    def prologue(x_ref, y_ref, xn_ref):
        def chunk(rs):
            inv = _row_inv_rms(x_ref[rs, :])
            xn_ref[rs, :] = (x_ref[rs, :] * inv * gpre_ref[...]).astype(BF16)
            y_ref[rs, :] = jnp.zeros((NORM_ROWS, y_ref.shape[1]), F32)

        _for_row_chunks(x_ref.shape[0], chunk)

    def epilogue(x_ref, y_ref):
        half_gain = 0.5 * gpost_ref[...]

        def chunk(rs):
            inv = _row_inv_rms(y_ref[rs, :])
            y_ref[rs, :] = x_ref[rs, :] + y_ref[rs, :] * inv * half_gain

        _for_row_chunks(x_ref.shape[0], chunk)

    def swiglu_part(xn, slot):
        gate = _dot(xn, wg_buf[slot].astype(BF16))
        up = _dot(xn, wu_buf[slot].astype(BF16))
        hidden = (gate / (1.0 + jnp.exp(-gate))) * up
        return _dot(hidden.astype(BF16), wd_buf[slot].astype(BF16))

    def accumulate(slot, with_sample):
        if with_sample:
            part = swiglu_part(jnp.concatenate([xnp_ref[...], xns_ref[...]], axis=0), slot)
            yp_ref[...] += part[:tm]
            ys_ref[...] += part[tm:]
        else:
            yp_ref[...] += swiglu_part(xnp_ref[...], slot)

    def ff_loop(with_sample):
        def pair(jj, carry):
            j = 2 * jj
            start(j + 1, 1)
            wait(j, 0)
            accumulate(0, with_sample)
            if with_sample:
                pl.when(jj < n_pairs - 1)(lambda: start(j + 2, 0))
            else:
                start(lax.rem(j + 2, n_ff_tiles), 0)
            wait(j + 1, 1)
            accumulate(1, with_sample)
            return carry

        lax.fori_loop(0, n_pairs, pair, 0)

    pl.when(row_tile == 0)(lambda: start(0, 0))
    prologue(xp_ref, yp_ref, xnp_ref)
    pl.when(last_tile)(lambda: prologue(xs_ref, ys_ref, xns_ref))
    pl.when(jnp.logical_not(last_tile))(lambda: ff_loop(False))
    pl.when(last_tile)(lambda: ff_loop(True))
    epilogue(xp_ref, yp_ref)
    pl.when(last_tile)(lambda: epilogue(xs_ref, ys_ref))


def _ffn(xp, xs, g_pre, w_gate, w_up, w_down, g_post):
    mp, d = xp.shape
    ms = xs.shape[0]
    d_ff = w_gate.shape[1]
    tm = FFN_ROW_TILE
    nj = d_ff // FF_TILE
    assert mp % tm == 0 and d_ff % FF_TILE == 0 and nj % 2 == 0
    prompt_row = lambda i: (i, 0)
    fixed = lambda i: (0, 0)
    in_hbm = pl.BlockSpec(memory_space=pl.ANY)
    return pl.pallas_call(
        functools.partial(_ffn_kernel, n_ff_tiles=nj),
        out_shape=(jax.ShapeDtypeStruct((mp, d), F32), jax.ShapeDtypeStruct((ms, d), F32)),
        grid=(mp // tm,),
        in_specs=[
            pl.BlockSpec((tm, d), prompt_row),
            pl.BlockSpec((ms, d), fixed),
            pl.BlockSpec((1, d), fixed),
            in_hbm, in_hbm, in_hbm,
            pl.BlockSpec((1, d), fixed),
        ],
        out_specs=(pl.BlockSpec((tm, d), prompt_row), pl.BlockSpec((ms, d), fixed)),
        scratch_shapes=[
            pltpu.VMEM((tm, d), BF16), pltpu.VMEM((ms, d), BF16),
            pltpu.VMEM((2, d, FF_TILE), F32), pltpu.VMEM((2, d, FF_TILE), F32),
            pltpu.VMEM((2, FF_TILE, d), F32),
            pltpu.SemaphoreType.DMA((3, 2)),
        ],
        compiler_params=_params(1),
        name="ffn",
    )(xp, xs, g_pre, w_gate, w_up, w_down, g_post)


def _rope_pairs(slab, cs):
    t = slab * cs
    return t + pltpu.roll(t, QK_ROPE, axis=1)


def _proj_kernel(hp_ref, hs_ref, csp_ref, css_ref, gmix_ref, win_ref, gq_ref, wq_ref, gkv_ref,
                 wkv_ref, wukt_ref, wp_ref, pscale_ref,
                 up_ref, ckvp_ref, kpep_ref, q_ref, k_ref, v_ref, poolp_ref,
                 us_ref, ckvs_ref, kpes_ref, qlat_ref, qpes_ref, ext_ref,
                 *, n_prompt_tiles, tiles_per_seq, n_heads, d_pool, q_lora, kv_lora):
    i = pl.program_id(0)

    @pl.when(i == 0)
    def _():
        ext_ref[...] = jnp.zeros(ext_ref.shape, F32)

    def common(h_ref, cs_ref, u_ref, ckv_ref, kpe_ref):
        hn = _rms(h_ref[...], gmix_ref[...]).astype(BF16)
        z = _dot_nt(hn, win_ref[...])
        u = z[:, :d_pool]
        u_ref[...] = u
        cqn = _rms(z[:, d_pool:d_pool + q_lora], gq_ref[...]).astype(BF16)
        ckv = _rms(z[:, d_pool + q_lora:d_pool + q_lora + kv_lora], gkv_ref[...])
        ckv_ref[...] = ckv
        cs = cs_ref[...]
        kpe = _rope_pairs(z[:, d_pool + q_lora + kv_lora:], cs)
        kpe_ref[...] = kpe[:, :QK_ROPE]
        qq = _dot(cqn, wq_ref[...])
        return u, ckv, kpe, qq, cs

    @pl.when(i < n_prompt_tiles)
    def _():
        u, ckv, kpe, qq, cs = common(hp_ref, csp_ref, up_ref, ckvp_ref, kpep_ref)
        _pool_rows(u, lax.rem(i, tiles_per_seq), wp_ref, pscale_ref, poolp_ref, ext_ref)
        lane = lax.broadcasted_iota(jnp.int32, kpe.shape, 1)
        kpe_pad = jnp.where(lane < QK_ROPE, kpe, 0.0).astype(BF16)
        kv = _dot(ckv.astype(BF16), wkv_ref[...])
        for h in range(n_heads):
            v_lo = n_heads * QK_NOPE + h * V_HEAD
            v_ref[h] = kv[:, v_lo:v_lo + V_HEAD].astype(BF16)
            slab = qq[:, h * HEAD_W:(h + 1) * HEAD_W]
            q_ref[h, :, :QK_NOPE] = slab[:, :QK_NOPE].astype(BF16)
            q_ref[h, :, QK_NOPE:] = _rope_pairs(slab[:, QK_NOPE:], cs).astype(BF16)
            k_ref[h, :, :QK_NOPE] = kv[:, h * QK_NOPE:(h + 1) * QK_NOPE].astype(BF16)
            k_ref[h, :, QK_NOPE:] = kpe_pad

    @pl.when(i == n_prompt_tiles)
    def _():
        _, _, _, qq, cs = common(hs_ref, css_ref, us_ref, ckvs_ref, kpes_ref)
        for h in range(n_heads):
            slab = qq[:, h * HEAD_W:(h + 1) * HEAD_W]
            qlat_ref[h] = _dot(slab[:, :QK_NOPE].astype(BF16), wukt_ref[h]).astype(BF16)
            qpes_ref[h] = _rope_pairs(slab[:, QK_NOPE:], cs)[:, :QK_ROPE].astype(BF16)


def _proj(hp, hs, cs_p, cs_s, g_mix, w_in2, g_q, w_q2, g_kv, w_kv2, w_uk_t, w_pool, pool_scale, *, seq, n_heads):
    mp, d = hp.shape
    ms = hs.shape[0]
    tm = PROJ_TILE
    ntp = mp // tm
    tiles_per_seq = seq // tm
    assert mp % tm == 0 and seq % tm == 0
    d_in2 = w_in2.shape[0]
    q_lora = w_q2.shape[0]
    kv_lora = w_kv2.shape[0]
    d_pool = d_in2 - q_lora - kv_lora - 2 * QK_ROPE
    prompt_row = lambda i: (jnp.minimum(i, ntp - 1), 0)
    prompt_head_row = lambda i: (0, jnp.minimum(i, ntp - 1), 0)
    fixed2 = lambda i: (0, 0)
    fixed3 = lambda i: (0, 0, 0)
    whole = lambda a: pl.BlockSpec(a.shape, fixed2 if a.ndim == 2 else fixed3)
    out_shape = (
        jax.ShapeDtypeStruct((mp, d_pool), F32),
        jax.ShapeDtypeStruct((mp, kv_lora), F32),
        jax.ShapeDtypeStruct((mp, QK_ROPE), F32),
        jax.ShapeDtypeStruct((n_heads, mp, HEAD_W), BF16),
        jax.ShapeDtypeStruct((n_heads, mp, HEAD_W), BF16),
        jax.ShapeDtypeStruct((n_heads, mp, V_HEAD), BF16),
        jax.ShapeDtypeStruct((mp, d_pool), BF16),
        jax.ShapeDtypeStruct((ms, d_pool), F32),
        jax.ShapeDtypeStruct((ms, kv_lora), F32),
        jax.ShapeDtypeStruct((ms, QK_ROPE), F32),
        jax.ShapeDtypeStruct((n_heads, ms, kv_lora), BF16),
        jax.ShapeDtypeStruct((n_heads, ms, QK_ROPE), BF16),
    )
    out_specs = (
        pl.BlockSpec((tm, d_pool), prompt_row),
        pl.BlockSpec((tm, kv_lora), prompt_row),
        pl.BlockSpec((tm, QK_ROPE), prompt_row),
        pl.BlockSpec((n_heads, tm, HEAD_W), prompt_head_row),
        pl.BlockSpec((n_heads, tm, HEAD_W), prompt_head_row),
        pl.BlockSpec((n_heads, tm, V_HEAD), prompt_head_row),
        pl.BlockSpec((tm, d_pool), prompt_row),
        pl.BlockSpec((ms, d_pool), fixed2),
        pl.BlockSpec((ms, kv_lora), fixed2),
        pl.BlockSpec((ms, QK_ROPE), fixed2),
        pl.BlockSpec((n_heads, ms, kv_lora), fixed3),
        pl.BlockSpec((n_heads, ms, QK_ROPE), fixed3),
    )
    return pl.pallas_call(
        functools.partial(_proj_kernel, n_prompt_tiles=ntp, tiles_per_seq=tiles_per_seq, n_heads=n_heads,
                          d_pool=d_pool, q_lora=q_lora, kv_lora=kv_lora),
        out_shape=out_shape,
        grid=(ntp + 1,),
        in_specs=[
            pl.BlockSpec((tm, d), prompt_row),
            pl.BlockSpec((ms, d), fixed2),
            pl.BlockSpec((tm, 2 * QK_ROPE), lambda i: (jnp.minimum(i, ntp - 1) % tiles_per_seq, 0)),
            whole(cs_s), whole(g_mix), whole(w_in2), whole(g_q), whole(w_q2), whole(g_kv),
            whole(w_kv2), whole(w_uk_t), whole(w_pool), whole(pool_scale),
        ],
        out_specs=out_specs,
        scratch_shapes=[pltpu.VMEM((tm + POOL_STATE + 1, d_pool), F32)],
        compiler_params=_params(1),
        name="proj",
    )(hp, hs, cs_p, cs_s, g_mix, w_in2, g_q, w_q2, g_kv, w_kv2, w_uk_t, w_pool, pool_scale)


def _pool_delta(window_sum, cur, inv_cnt):
    return window_sum * inv_cnt - cur


def _pool_rows(u, seq_tile, wp_ref, scale_ref, o_ref, ext_ref):
    rows, d_pool = u.shape
    group = d_pool // len(POOL_WINDOWS)
    halo = POOL_STATE + 1

    ext_ref[:halo, :] = jnp.where(seq_tile == 0, 0.0, ext_ref[rows:rows + halo, :])
    ext_ref[halo:, :] = u
    pos = seq_tile * rows + lax.broadcasted_iota(jnp.int32, (rows, 1), 0)
    for g, w in enumerate(POOL_WINDOWS):
        cols = slice(g * group, (g + 1) * group)
        cur = ext_ref[halo:, cols]
        acc = cur
        for k in range(1, w):
            acc = acc + ext_ref[halo - k:halo - k + rows, cols]
        inv_cnt = 1.0 / jnp.minimum(pos + 1, w).astype(F32)
        delta = _pool_delta(acc, cur, inv_cnt)
        out = _dot(delta.astype(BF16), wp_ref[g])
        o_ref[:, cols] = (out * scale_ref[:, cols]).astype(BF16)


def _attn_kernel(q_ref, k_ref, v_ref, o_ref, *, n_heads, n_q_tiles, exp2_scale):
    qi = pl.program_id(1)
    t = ATTN_TILE
    row = lax.broadcasted_iota(jnp.int32, (t, t), 0)
    col = lax.broadcasted_iota(jnp.int32, (t, t), 1)
    causal = row >= col

    def tile(n_past):
        def head(h, carry):
            q = q_ref[h]
            s_diag = jnp.where(causal, _dot_nt(q, k_ref[h, n_past:n_past + t, :]), -jnp.inf)
            m = jnp.max(s_diag, axis=1, keepdims=True)
            if n_past:
                s_past = _dot_nt(q, k_ref[h, :n_past, :])
                m = jnp.maximum(m, jnp.max(s_past, axis=1, keepdims=True))
            p_diag = jnp.exp2((s_diag - m) * exp2_scale)
            l = jnp.sum(p_diag, axis=1, keepdims=True)
            o = _dot(p_diag.astype(BF16), v_ref[h, n_past:n_past + t, :])
            if n_past:
                p_past = jnp.exp2((s_past - m) * exp2_scale)
                l = l + jnp.sum(p_past, axis=1, keepdims=True)
                o = o + _dot(p_past.astype(BF16), v_ref[h, :n_past, :])
            o_ref[h] = (o / l).astype(BF16)
            return carry

        lax.fori_loop(0, n_heads, head, 0, unroll=ATTN_HEADS_IN_FLIGHT)

    for kq in range(n_q_tiles):
        pl.when(qi == kq)(functools.partial(tile, kq * t))


def _attn(q, k, v, *, batch, seq, sm_scale):
    n_heads, m, _ = q.shape
    t = ATTN_TILE
    nq = seq // t
    assert seq % t == 0
    return pl.pallas_call(
        functools.partial(_attn_kernel, n_heads=n_heads, n_q_tiles=nq,
                          exp2_scale=sm_scale * math.log2(math.e)),
        out_shape=jax.ShapeDtypeStruct((n_heads, m, V_HEAD), BF16),
        grid=(batch, nq),
        in_specs=[
            pl.BlockSpec((n_heads, t, HEAD_W), lambda b, i: (0, b * nq + i, 0)),
            pl.BlockSpec((n_heads, seq, HEAD_W), lambda b, i: (0, b, 0)),
            pl.BlockSpec((n_heads, seq, V_HEAD), lambda b, i: (0, b, 0)),
        ],
        out_specs=pl.BlockSpec((n_heads, t, V_HEAD), lambda b, i: (0, b * nq + i, 0)),
        compiler_params=_params(2),
        name="attn",
    )(q, k, v)


def _decode_kernel(pt_ref, qlat_ref, qpe_ref, ckvs_ref, kpes_ref, kv_hbm, rope_hbm, o_ref,
                   kv_buf, rope_buf, kv_sem, rope_sem, *, n_pages, sm_scale):
    b = pl.program_id(0)
    n_samples = pl.num_programs(0)
    slot = b % 2

    def page_copies(sample, sl):
        copies = []
        for i in range(n_pages):
            page_id = pt_ref[sample * n_pages + i]
            copies.append(pltpu.make_async_copy(kv_hbm.at[page_id], kv_buf.at[sl, i], kv_sem.at[sl]))
            copies.append(pltpu.make_async_copy(rope_hbm.at[page_id], rope_buf.at[sl, i], rope_sem.at[sl]))
        return copies

    @pl.when(b == 0)
    def _():
        for c in page_copies(0, 0):
            c.start()

    @pl.when(b + 1 < n_samples)
    def _():
        for c in page_copies(b + 1, 1 - slot):
            c.start()

    for c in page_copies(b, slot):
        c.wait()

    qlat = qlat_ref[b]
    qpe = qpe_ref[b]
    own_ckv = ckvs_ref[pl.ds(b, 1), :].astype(BF16).astype(F32)
    own_kpe = kpes_ref[pl.ds(b, 1), :].astype(BF16).astype(F32)
    s_own = (jnp.sum(qlat.astype(F32) * own_ckv, axis=1, keepdims=True)
             + jnp.sum(qpe.astype(F32) * own_kpe, axis=1, keepdims=True)) * sm_scale

    chunk = DECODE_CHUNK_PAGES
    ckv_parts, s_parts = [], []
    for c0 in range(0, n_pages, chunk):
        ckv = jnp.concatenate([kv_buf[slot, i] for i in range(c0, c0 + chunk)], axis=0).astype(BF16)
        kpe_t = jnp.concatenate([rope_buf[slot, i] for i in range(c0, c0 + chunk)], axis=1).astype(BF16)
        ckv_parts.append(ckv)
        s_parts.append((_dot_nt(qlat, ckv) + _dot(qpe, kpe_t)) * sm_scale)
    m = s_own
    for s in s_parts:
        m = jnp.maximum(m, jnp.max(s, axis=1, keepdims=True))
    p_own = jnp.exp(s_own - m)
    l = p_own
    acc = p_own.astype(BF16).astype(F32) * own_ckv
    for s, ckv in zip(s_parts, ckv_parts):
        p = jnp.exp(s - m)
        l = l + jnp.sum(p, axis=1, keepdims=True)
        acc = acc + _dot(p.astype(BF16), ckv)
    o_ref[b] = acc / l


def _decode(page_table, q_lat, q_pe, ckv_own, kpe_own, cache_kv, cache_rope_t, *, sm_scale):
    bs, n_heads, kv_lora = q_lat.shape
    n_pages = page_table.shape[1]
    page = cache_kv.shape[1]
    assert n_pages % DECODE_CHUNK_PAGES == 0
    whole = lambda a: pl.BlockSpec(a.shape, lambda b, pt, nd=a.ndim: (0,) * nd)
    grid_spec = pltpu.PrefetchScalarGridSpec(
        num_scalar_prefetch=1,
        grid=(bs,),
        in_specs=[whole(q_lat), whole(q_pe), whole(ckv_own), whole(kpe_own),
                  pl.BlockSpec(memory_space=pl.ANY), pl.BlockSpec(memory_space=pl.ANY)],
        out_specs=pl.BlockSpec((bs, n_heads, kv_lora), lambda b, pt: (0, 0, 0)),
        scratch_shapes=[
            pltpu.VMEM((2, n_pages, page, kv_lora), F32),
            pltpu.VMEM((2, n_pages, QK_ROPE, page), F32),
            pltpu.SemaphoreType.DMA((2,)),
            pltpu.SemaphoreType.DMA((2,)),
        ],
    )
    return pl.pallas_call(
        functools.partial(_decode_kernel, n_pages=n_pages, sm_scale=sm_scale),
        out_shape=jax.ShapeDtypeStruct((bs, n_heads, kv_lora), F32),
        grid_spec=grid_spec,
        compiler_params=_params(1),
        name="decode",
    )(page_table.reshape(-1), q_lat, q_pe, ckv_own, kpe_own, cache_kv, cache_rope_t)


def _sample_kernel(olat_ref, wuv_ref, sp_ref, us_ref, wp_ref, scale_ref, pool_ref, attn_ref,
                   *, n_heads, group, past_len):
    for h in range(n_heads):
        attn_ref[:, h * V_HEAD:(h + 1) * V_HEAD] = _dot(olat_ref[h].astype(BF16), wuv_ref[h]).astype(BF16)
    for g, w in enumerate(POOL_WINDOWS):
        cols = slice(g * group, (g + 1) * group)
        cur = us_ref[:, cols]
        acc = cur
        for k in range(1, w):
            acc = acc + sp_ref[POOL_STATE - k, :, cols]
        delta = _pool_delta(acc, cur, 1.0 / min(past_len + 1, w))
        out = _dot(delta.astype(BF16), wp_ref[g])
        pool_ref[:, cols] = (out * scale_ref[:, cols]).astype(BF16)


def _sample_mix(o_lat, w_uv_h, state_pool, u_s, w_pool, pool_scale, *, past_len):
    n_heads, ms, _ = o_lat.shape
    d_pool = u_s.shape[1]
    return pl.pallas_call(
        functools.partial(_sample_kernel, n_heads=n_heads, group=d_pool // len(POOL_WINDOWS),
                          past_len=past_len),
        out_shape=(jax.ShapeDtypeStruct((ms, d_pool), BF16),
                   jax.ShapeDtypeStruct((ms, n_heads * V_HEAD), BF16)),
        compiler_params=pltpu.CompilerParams(vmem_limit_bytes=VMEM_LIMIT),
        name="sample_mix",
    )(o_lat, w_uv_h, state_pool, u_s, w_pool, pool_scale)


def _outproj_kernel(hp_ref, poolp_ref, attnp_ref, hs_ref, pools_ref, attns_ref, wo_ref, g_ref,
                    yp_ref, ys_ref, *, n_prompt_tiles, d_pool):
    i = pl.program_id(0)

    def body(h_ref, pool_ref, attn, y_ref):
        mix = _dot(pool_ref[...], wo_ref[:d_pool, :]) + _dot(attn, wo_ref[d_pool:, :])
        y_ref[...] = h_ref[...] + _rms(mix, g_ref[...])

    @pl.when(i < n_prompt_tiles)
    def _():
        heads = [attnp_ref[h] for h in range(attnp_ref.shape[0])]
        body(hp_ref, poolp_ref, jnp.concatenate(heads, axis=1), yp_ref)

    @pl.when(i == n_prompt_tiles)
    def _():
        body(hs_ref, pools_ref, attns_ref[...], ys_ref)


def _outproj(hp, pool_p, attn_p, hs, pool_s, attn_s, w_out, g_post):
    mp, d = hp.shape
    ms = hs.shape[0]
    d_pool = pool_p.shape[1]
    n_heads, _, v_head = attn_p.shape
    tm = ROW_TILE
    ntp = mp // tm
    prompt_row = lambda i: (jnp.minimum(i, ntp - 1), 0)
    fixed = lambda i: (0, 0)
    return pl.pallas_call(
        functools.partial(_outproj_kernel, n_prompt_tiles=ntp, d_pool=d_pool),
        out_shape=(jax.ShapeDtypeStruct((mp, d), F32), jax.ShapeDtypeStruct((ms, d), F32)),
        grid=(ntp + 1,),
        in_specs=[
            pl.BlockSpec((tm, d), prompt_row),
            pl.BlockSpec((tm, d_pool), prompt_row),
            pl.BlockSpec((n_heads, tm, v_head), lambda i: (0, jnp.minimum(i, ntp - 1), 0)),
            pl.BlockSpec((ms, d), fixed),
            pl.BlockSpec((ms, d_pool), fixed),
            pl.BlockSpec((ms, n_heads * v_head), fixed),
            pl.BlockSpec(w_out.shape, fixed),
            pl.BlockSpec((1, d), fixed),
        ],
        out_specs=(pl.BlockSpec((tm, d), prompt_row), pl.BlockSpec((ms, d), fixed)),
        compiler_params=_params(1),
        name="outproj",
    )(hp, pool_p, attn_p, hs, pool_s, attn_s, w_out, g_post)


def _rotate_half_columns(w):
    half = QK_ROPE // 2
    return jnp.concatenate([-w[..., half:], w[..., :half]], axis=-1)


def _rope_table(pos):
    half = QK_ROPE // 2
    inv = ROPE_THETA ** (-jnp.arange(half, dtype=F32) / half)
    ang = pos.astype(F32)[:, None] * inv[None, :]
    cos, sin = jnp.cos(ang), jnp.sin(ang)
    return jnp.concatenate([cos, cos, sin, sin], axis=-1)


def kernel(x_prompt, x_sample, cache_kv_latent, cache_k_rope, state_pool, page_table,
           g_ffn1_pre, w1_gate, w1_up, w1_down, g_ffn1_post,
           g_mix_pre, w_in, w_pool, pool_scale, g_q, w_uq, g_kv, w_uk, w_uv, w_out, g_mix_post,
           g_ffn2_pre, w2_gate, w2_up, w2_down, g_ffn2_post):
    depth = w_in.shape[0]
    bp, sp, d = x_prompt.shape
    bs, ts, _ = x_sample.shape
    n_pages = page_table.shape[1]
    page = cache_kv_latent.shape[2]
    past_len = n_pages * page
    kv_lora, n_heads, _ = w_uk.shape[1:]
    q_lora = w_uq.shape[1]
    d_pool = w_pool.shape[1] * w_pool.shape[2]
    assert ts == 1, "the decode kernel handles one new token per sample sequence"
    sm_scale = (QK_NOPE + QK_ROPE) ** -0.5

    cs_p = _rope_table(jnp.arange(sp, dtype=jnp.int32))
    cs_s = jnp.tile(_rope_table(past_len + jnp.arange(ts, dtype=jnp.int32)), (bs, 1))

    hp = x_prompt.reshape(bp * sp, d)
    hs = x_sample.reshape(bs * ts, d)
    outs = [[] for _ in range(6)]
    for l in range(depth):
        row = lambda g: g[l][None, :]
        w_in_t = jnp.swapaxes(w_in[l], 0, 1)
        w_rope_t = jnp.swapaxes(_rotate_half_columns(w_in[l][:, d_pool + q_lora + kv_lora:]), 0, 1)
        w_in2 = jnp.concatenate([w_in_t, w_rope_t], axis=0).astype(BF16)
        wq = w_uq[l].reshape(q_lora, n_heads, QK_NOPE + QK_ROPE)
        w_q2 = jnp.concatenate(
            [wq, _rotate_half_columns(wq[..., QK_NOPE:])], axis=-1).reshape(q_lora, n_heads * HEAD_W).astype(BF16)
        w_kv2 = jnp.concatenate(
            [w_uk[l].reshape(kv_lora, -1), w_uv[l].reshape(kv_lora, -1)], axis=1).astype(BF16)
        w_uk_t = jnp.transpose(w_uk[l], (1, 2, 0)).astype(BF16)
        w_uv_h = jnp.transpose(w_uv[l], (1, 0, 2)).astype(BF16)
        w_pool_l = w_pool[l].astype(BF16)
        w_out_l = w_out[l].astype(BF16)

        hp, hs = _ffn(hp, hs, row(g_ffn1_pre), w1_gate[l], w1_up[l], w1_down[l], row(g_ffn1_post))
        (u_p, ckv_p, kpe_p, q_p, k_p, v_p, pool_p, u_s, ckv_s, kpe_s, qlat_s, qpe_s) = _proj(
            hp, hs, cs_p, cs_s, row(g_mix_pre), w_in2, row(g_q), w_q2, row(g_kv), w_kv2, w_uk_t,
            w_pool_l, row(pool_scale), seq=sp, n_heads=n_heads)
        attn_p = _attn(q_p, k_p, v_p, batch=bp, seq=sp, sm_scale=sm_scale)
        o_lat = _decode(page_table, jnp.transpose(qlat_s, (1, 0, 2)), jnp.transpose(qpe_s, (1, 0, 2)),
                        ckv_s, kpe_s, cache_kv_latent[l],
                        jnp.swapaxes(cache_k_rope[l], 1, 2),
                        sm_scale=sm_scale)
        state_t = jnp.transpose(state_pool[l], (1, 0, 2))
        pool_s, attn_s = _sample_mix(jnp.transpose(o_lat, (1, 0, 2)), w_uv_h, state_t, u_s,
                                     w_pool_l, row(pool_scale), past_len=past_len)
        hp, hs = _outproj(hp, pool_p, attn_p, hs, pool_s, attn_s, w_out_l, row(g_mix_post))
        hp, hs = _ffn(hp, hs, row(g_ffn2_pre), w2_gate[l], w2_up[l], w2_down[l], row(g_ffn2_post))

        u_p3 = u_p.reshape(bp, sp, d_pool)
        outs[0].append(ckv_p.reshape(bp, sp, kv_lora))
        outs[1].append(kpe_p.reshape(bp, sp, QK_ROPE))
        outs[2].append(u_p3[:, sp - POOL_STATE:])
        outs[3].append(ckv_s.reshape(bs, ts, kv_lora))
        outs[4].append(kpe_s.reshape(bs, ts, QK_ROPE))
        new_rows_t = jnp.transpose(u_s.reshape(bs, ts, d_pool), (1, 0, 2))
        outs[5].append(jnp.transpose(jnp.concatenate([state_t[ts:], new_rows_t], axis=0), (1, 0, 2)))
    return (hp.reshape(bp, sp, d), hs.reshape(bs, ts, d), *[jnp.stack(o) for o in outs])
```

```python
import functools
import math

import jax
import jax.numpy as jnp
from jax import lax
from jax.experimental import pallas as pl
from jax.experimental.pallas import tpu as pltpu

F32 = jnp.float32
BF16 = jnp.bfloat16

EPS = 1e-6
ROPE_THETA = 10000.0
POOL_WINDOWS = (2, 4, 8, 16)
POOL_STATE = max(POOL_WINDOWS) - 1
QK_NOPE = 128
QK_ROPE = 64
V_HEAD = 128
HEAD_W = 2 * 128

V7X_VMEM_BYTES = 64 * 1024 * 1024
VMEM_LIMIT = V7X_VMEM_BYTES - 4 * 1024 * 1024

ROW_TILE = 512
FFN_ROW_TILE = 1024
FF_TILE = 256
NORM_ROWS = 128
PROJ_TILE = 512
ATTN_TILE = 512
ATTN_HEADS_IN_FLIGHT = 8
DECODE_CHUNK_PAGES = 32


def _params(n_grid_dims):
    return pltpu.CompilerParams(
        dimension_semantics=("arbitrary",) * n_grid_dims, vmem_limit_bytes=VMEM_LIMIT)


def _rms(x, g):
    return x * lax.rsqrt(jnp.mean(x * x, axis=-1, keepdims=True) + EPS) * g


def _dot(a, b):
    return jnp.dot(a, b, preferred_element_type=F32)


def _dot_nt(a, b):
    return lax.dot_general(a, b, (((1,), (1,)), ((), ())), preferred_element_type=F32)


def _for_row_chunks(rows, fn):
    n_chunks, rem = divmod(rows, NORM_ROWS)
    assert rem == 0

    def step(c, carry):
        fn(pl.ds(pl.multiple_of(c * NORM_ROWS, NORM_ROWS), NORM_ROWS))
        return carry

    lax.fori_loop(0, n_chunks, step, 0, unroll=2 if n_chunks % 2 == 0 else 1)


def _row_inv_rms(x):
    return lax.rsqrt(jnp.mean(x * x, axis=-1, keepdims=True) + EPS)


def _ffn_kernel(xp_ref, xs_ref, gpre_ref, wg_hbm, wu_hbm, wd_hbm, gpost_ref,
                yp_ref, ys_ref, xnp_ref, xns_ref, wg_buf, wu_buf, wd_buf, w_sem, *, n_ff_tiles):
    row_tile = pl.program_id(0)
    last_tile = row_tile == pl.num_programs(0) - 1
    tm = xp_ref.shape[0]
    n_pairs = n_ff_tiles // 2

    def weight_copies(j, slot):
        cols = pl.ds(pl.multiple_of(j * FF_TILE, FF_TILE), FF_TILE)
        return (pltpu.make_async_copy(wg_hbm.at[:, cols], wg_buf.at[slot], w_sem.at[0, slot]),
                pltpu.make_async_copy(wu_hbm.at[:, cols], wu_buf.at[slot], w_sem.at[1, slot]),
                pltpu.make_async_copy(wd_hbm.at[cols, :], wd_buf.at[slot], w_sem.at[2, slot]))

    def start(j, slot):
        for c in weight_copies(j, slot):
            c.start()

    def wait(j, slot):
        for c in weight_copies(j, slot):
            c.wait()

    def prologue(x_ref, y_ref, xn_ref):
        def chunk(rs):
            inv = _row_inv_rms(x_ref[rs, :])
            xn_ref[rs, :] = (x_ref[rs, :] * inv * gpre_ref[...]).astype(BF16)
            y_ref[rs, :] = jnp.zeros((NORM_ROWS, y_ref.shape[1]), F32)

        _for_row_chunks(x_ref.shape[0], chunk)

    def epilogue(x_ref, y_ref):
        half_gain = 0.5 * gpost_ref[...]

        def chunk(rs):
            inv = _row_inv_rms(y_ref[rs, :])
            y_ref[rs, :] = x_ref[rs, :] + y_ref[rs, :] * inv * half_gain

        _for_row_chunks(x_ref.shape[0], chunk)

    def swiglu_part(xn, slot):
        gate = _dot(xn, wg_buf[slot].astype(BF16))
        up = _dot(xn, wu_buf[slot].astype(BF16))
        hidden = (gate / (1.0 + jnp.exp(-gate))) * up
        return _dot(hidden.astype(BF16), wd_buf[slot].astype(BF16))

    def accumulate(slot, with_sample):
        if with_sample:
            part = swiglu_part(jnp.concatenate([xnp_ref[...], xns_ref[...]], axis=0), slot)
            yp_ref[...] += part[:tm]
            ys_ref[...] += part[tm:]
        else:
            yp_ref[...] += swiglu_part(xnp_ref[...], slot)

    def ff_loop(with_sample):
        def pair(jj, carry):
            j = 2 * jj
            start(j + 1, 1)
            wait(j, 0)
            accumulate(0, with_sample)
            if with_sample:
                pl.when(jj < n_pairs - 1)(lambda: start(j + 2, 0))
            else:
                start(lax.rem(j + 2, n_ff_tiles), 0)
            wait(j + 1, 1)
            accumulate(1, with_sample)
            return carry

        lax.fori_loop(0, n_pairs, pair, 0)

    pl.when(row_tile == 0)(lambda: start(0, 0))
    prologue(xp_ref, yp_ref, xnp_ref)
    pl.when(last_tile)(lambda: prologue(xs_ref, ys_ref, xns_ref))
    pl.when(jnp.logical_not(last_tile))(lambda: ff_loop(False))
    pl.when(last_tile)(lambda: ff_loop(True))
    epilogue(xp_ref, yp_ref)
    pl.when(last_tile)(lambda: epilogue(xs_ref, ys_ref))


def _ffn(xp, xs, g_pre, w_gate, w_up, w_down, g_post):
    mp, d = xp.shape
    ms = xs.shape[0]
    d_ff = w_gate.shape[1]
    tm = FFN_ROW_TILE
    nj = d_ff // FF_TILE
    assert mp % tm == 0 and d_ff % FF_TILE == 0 and nj % 2 == 0
    prompt_row = lambda i: (i, 0)
    fixed = lambda i: (0, 0)
    in_hbm = pl.BlockSpec(memory_space=pl.ANY)
    return pl.pallas_call(
        functools.partial(_ffn_kernel, n_ff_tiles=nj),
        out_shape=(jax.ShapeDtypeStruct((mp, d), F32), jax.ShapeDtypeStruct((ms, d), F32)),
        grid=(mp // tm,),
        in_specs=[
            pl.BlockSpec((tm, d), prompt_row),
            pl.BlockSpec((ms, d), fixed),
            pl.BlockSpec((1, d), fixed),
            in_hbm, in_hbm, in_hbm,
            pl.BlockSpec((1, d), fixed),
        ],
        out_specs=(pl.BlockSpec((tm, d), prompt_row), pl.BlockSpec((ms, d), fixed)),
        scratch_shapes=[
            pltpu.VMEM((tm, d), BF16), pltpu.VMEM((ms, d), BF16),
            pltpu.VMEM((2, d, FF_TILE), F32), pltpu.VMEM((2, d, FF_TILE), F32),
            pltpu.VMEM((2, FF_TILE, d), F32),
            pltpu.SemaphoreType.DMA((3, 2)),
        ],
        compiler_params=_params(1),
        name="ffn",
    )(xp, xs, g_pre, w_gate, w_up, w_down, g_post)


def _rope_pairs(slab, cs):
    t = slab * cs
    return t + pltpu.roll(t, QK_ROPE, axis=1)


def _proj_kernel(hp_ref, hs_ref, csp_ref, css_ref, gmix_ref, win_ref, gq_ref, wq_ref, gkv_ref,
                 wkv_ref, wukt_ref, wp_ref, pscale_ref,
                 up_ref, ckvp_ref, kpep_ref, q_ref, k_ref, v_ref, poolp_ref,
                 us_ref, ckvs_ref, kpes_ref, qlat_ref, qpes_ref, ext_ref,
                 *, n_prompt_tiles, tiles_per_seq, n_heads, d_pool, q_lora, kv_lora):
    i = pl.program_id(0)

    @pl.when(i == 0)
    def _():
        ext_ref[...] = jnp.zeros(ext_ref.shape, F32)

    def common(h_ref, cs_ref, u_ref, ckv_ref):
        hn = _rms(h_ref[...], gmix_ref[...]).astype(BF16)
        z = _dot_nt(hn, win_ref[...])
        u = z[:, :d_pool]
        u_ref[...] = u
        cqn = _rms(z[:, d_pool:d_pool + q_lora], gq_ref[...]).astype(BF16)
        ckv = _rms(z[:, d_pool + q_lora:d_pool + q_lora + kv_lora], gkv_ref[...])
        ckv_ref[...] = ckv
        cs = cs_ref[...]
        kpe = _rope_pairs(z[:, d_pool + q_lora + kv_lora:], cs)
        qq = _dot(cqn, wq_ref[...])
        return u, ckv, kpe, qq, cs

    @pl.when(i < n_prompt_tiles)
    def _():
        u, ckv, kpe, qq, cs = common(hp_ref, csp_ref, up_ref, ckvp_ref)
        kpep_ref[0] = kpe.T[:QK_ROPE, :]
        _pool_rows(u, lax.rem(i, tiles_per_seq), wp_ref, pscale_ref, poolp_ref, ext_ref)
        lane = lax.broadcasted_iota(jnp.int32, kpe.shape, 1)
        kpe_pad = jnp.where(lane < QK_ROPE, kpe, 0.0).astype(BF16)
        kv = _dot(ckv.astype(BF16), wkv_ref[...])
        for h in range(n_heads):
            v_lo = n_heads * QK_NOPE + h * V_HEAD
            v_ref[h] = kv[:, v_lo:v_lo + V_HEAD].astype(BF16)
            slab = qq[:, h * HEAD_W:(h + 1) * HEAD_W]
            q_ref[h, :, :QK_NOPE] = slab[:, :QK_NOPE].astype(BF16)
            q_ref[h, :, QK_NOPE:] = _rope_pairs(slab[:, QK_NOPE:], cs).astype(BF16)
            k_ref[h, :, :QK_NOPE] = kv[:, h * QK_NOPE:(h + 1) * QK_NOPE].astype(BF16)
            k_ref[h, :, QK_NOPE:] = kpe_pad

    @pl.when(i == n_prompt_tiles)
    def _():
        _, _, kpe, qq, cs = common(hs_ref, css_ref, us_ref, ckvs_ref)
        kpes_ref[...] = kpe[:, :QK_ROPE]
        for h in range(n_heads):
            slab = qq[:, h * HEAD_W:(h + 1) * HEAD_W]
            qlat_ref[h] = _dot(slab[:, :QK_NOPE].astype(BF16), wukt_ref[h]).astype(BF16)
            qpes_ref[h] = _rope_pairs(slab[:, QK_NOPE:], cs)[:, :QK_ROPE].astype(BF16)


def _proj(hp, hs, cs_p, cs_s, g_mix, w_in2, g_q, w_q2, g_kv, w_kv2, w_uk_t, w_pool, pool_scale, *, seq, n_heads):
    mp, d = hp.shape
    ms = hs.shape[0]
    tm = PROJ_TILE
    ntp = mp // tm
    tiles_per_seq = seq // tm
    assert mp % tm == 0 and seq % tm == 0
    d_in2 = w_in2.shape[0]
    q_lora = w_q2.shape[0]
    kv_lora = w_kv2.shape[0]
    d_pool = d_in2 - q_lora - kv_lora - 2 * QK_ROPE
    prompt_row = lambda i: (jnp.minimum(i, ntp - 1), 0)
    prompt_head_row = lambda i: (0, jnp.minimum(i, ntp - 1), 0)
    fixed2 = lambda i: (0, 0)
    fixed3 = lambda i: (0, 0, 0)
    whole = lambda a: pl.BlockSpec(a.shape, fixed2 if a.ndim == 2 else fixed3)
    out_shape = (
        jax.ShapeDtypeStruct((mp, d_pool), F32),
        jax.ShapeDtypeStruct((mp, kv_lora), F32),
        jax.ShapeDtypeStruct((mp // seq, QK_ROPE, seq), F32),
        jax.ShapeDtypeStruct((n_heads, mp, HEAD_W), BF16),
        jax.ShapeDtypeStruct((n_heads, mp, HEAD_W), BF16),
        jax.ShapeDtypeStruct((n_heads, mp, V_HEAD), BF16),
        jax.ShapeDtypeStruct((mp, d_pool), BF16),
        jax.ShapeDtypeStruct((ms, d_pool), F32),
        jax.ShapeDtypeStruct((ms, kv_lora), F32),
        jax.ShapeDtypeStruct((ms, QK_ROPE), F32),
        jax.ShapeDtypeStruct((n_heads, ms, kv_lora), BF16),
        jax.ShapeDtypeStruct((n_heads, ms, QK_ROPE), BF16),
    )
    out_specs = (
        pl.BlockSpec((tm, d_pool), prompt_row),
        pl.BlockSpec((tm, kv_lora), prompt_row),
        pl.BlockSpec((1, QK_ROPE, tm), lambda i: (jnp.minimum(i, ntp - 1) // tiles_per_seq, 0,
                                                  jnp.minimum(i, ntp - 1) % tiles_per_seq)),
        pl.BlockSpec((n_heads, tm, HEAD_W), prompt_head_row),
        pl.BlockSpec((n_heads, tm, HEAD_W), prompt_head_row),
        pl.BlockSpec((n_heads, tm, V_HEAD), prompt_head_row),
        pl.BlockSpec((tm, d_pool), prompt_row),
        pl.BlockSpec((ms, d_pool), fixed2),
        pl.BlockSpec((ms, kv_lora), fixed2),
        pl.BlockSpec((ms, QK_ROPE), fixed2),
        pl.BlockSpec((n_heads, ms, kv_lora), fixed3),
        pl.BlockSpec((n_heads, ms, QK_ROPE), fixed3),
    )
    return pl.pallas_call(
        functools.partial(_proj_kernel, n_prompt_tiles=ntp, tiles_per_seq=tiles_per_seq, n_heads=n_heads,
                          d_pool=d_pool, q_lora=q_lora, kv_lora=kv_lora),
        out_shape=out_shape,
        grid=(ntp + 1,),
        in_specs=[
            pl.BlockSpec((tm, d), prompt_row),
            pl.BlockSpec((ms, d), fixed2),
            pl.BlockSpec((tm, 2 * QK_ROPE), lambda i: (jnp.minimum(i, ntp - 1) % tiles_per_seq, 0)),
            whole(cs_s), whole(g_mix), whole(w_in2), whole(g_q), whole(w_q2), whole(g_kv),
            whole(w_kv2), whole(w_uk_t), whole(w_pool), whole(pool_scale),
        ],
        out_specs=out_specs,
        scratch_shapes=[pltpu.VMEM((tm + POOL_STATE + 1, d_pool), F32)],
        compiler_params=_params(1),
        name="proj",
    )(hp, hs, cs_p, cs_s, g_mix, w_in2, g_q, w_q2, g_kv, w_kv2, w_uk_t, w_pool, pool_scale)


def _pool_delta(window_sum, cur, inv_cnt):
    return window_sum * inv_cnt - cur


def _pool_rows(u, seq_tile, wp_ref, scale_ref, o_ref, ext_ref):
    rows, d_pool = u.shape
    group = d_pool // len(POOL_WINDOWS)
    halo = POOL_STATE + 1

    ext_ref[:halo, :] = jnp.where(seq_tile == 0, 0.0, ext_ref[rows:rows + halo, :])
    ext_ref[halo:, :] = u
    pos = seq_tile * rows + lax.broadcasted_iota(jnp.int32, (rows, 1), 0)
    for g, w in enumerate(POOL_WINDOWS):
        cols = slice(g * group, (g + 1) * group)
        cur = ext_ref[halo:, cols]
        acc = cur
        for k in range(1, w):
            acc = acc + ext_ref[halo - k:halo - k + rows, cols]
        inv_cnt = 1.0 / jnp.minimum(pos + 1, w).astype(F32)
        delta = _pool_delta(acc, cur, inv_cnt)
        out = _dot(delta.astype(BF16), wp_ref[g])
        o_ref[:, cols] = (out * scale_ref[:, cols]).astype(BF16)


def _attn_kernel(q_ref, k_ref, v_ref, o_ref, *, n_heads, n_q_tiles, exp2_scale):
    qi = pl.program_id(1)
    t = ATTN_TILE
    row = lax.broadcasted_iota(jnp.int32, (t, t), 0)
    col = lax.broadcasted_iota(jnp.int32, (t, t), 1)
    causal = row >= col

    def tile(n_past):
        def head(h, carry):
            q = q_ref[h]
            s_diag = jnp.where(causal, _dot_nt(q, k_ref[h, n_past:n_past + t, :]), -jnp.inf)
            m = jnp.max(s_diag, axis=1, keepdims=True)
            if n_past:
                s_past = _dot_nt(q, k_ref[h, :n_past, :])
                m = jnp.maximum(m, jnp.max(s_past, axis=1, keepdims=True))
            p_diag = jnp.exp2((s_diag - m) * exp2_scale)
            l = jnp.sum(p_diag, axis=1, keepdims=True)
            o = _dot(p_diag.astype(BF16), v_ref[h, n_past:n_past + t, :])
            if n_past:
                p_past = jnp.exp2((s_past - m) * exp2_scale)
                l = l + jnp.sum(p_past, axis=1, keepdims=True)
                o = o + _dot(p_past.astype(BF16), v_ref[h, :n_past, :])
            o_ref[h] = (o / l).astype(BF16)
            return carry

        lax.fori_loop(0, n_heads, head, 0, unroll=ATTN_HEADS_IN_FLIGHT)

    for kq in range(n_q_tiles):
        pl.when(qi == kq)(functools.partial(tile, kq * t))


def _attn(q, k, v, *, batch, seq, sm_scale):
    n_heads, m, _ = q.shape
    t = ATTN_TILE
    nq = seq // t
    assert seq % t == 0
    return pl.pallas_call(
        functools.partial(_attn_kernel, n_heads=n_heads, n_q_tiles=nq,
                          exp2_scale=sm_scale * math.log2(math.e)),
        out_shape=jax.ShapeDtypeStruct((n_heads, m, V_HEAD), BF16),
        grid=(batch, nq),
        in_specs=[
            pl.BlockSpec((n_heads, t, HEAD_W), lambda b, i: (0, b * nq + i, 0)),
            pl.BlockSpec((n_heads, seq, HEAD_W), lambda b, i: (0, b, 0)),
            pl.BlockSpec((n_heads, seq, V_HEAD), lambda b, i: (0, b, 0)),
        ],
        out_specs=pl.BlockSpec((n_heads, t, V_HEAD), lambda b, i: (0, b * nq + i, 0)),
        compiler_params=_params(2),
        name="attn",
    )(q, k, v)


def _decode_kernel(pt_ref, qlat_ref, qpe_ref, ckvs_ref, kpes_ref, kv_hbm, rope_hbm, o_ref,
                   kv_buf, rope_buf, kv_sem, rope_sem, *, n_pages, sm_scale):
    b = pl.program_id(0)
    n_samples = pl.num_programs(0)
    slot = b % 2

    def page_copies(sample, sl):
        copies = []
        for i in range(n_pages):
            page_id = pt_ref[sample * n_pages + i]
            copies.append(pltpu.make_async_copy(kv_hbm.at[page_id], kv_buf.at[sl, i], kv_sem.at[sl]))
            copies.append(pltpu.make_async_copy(rope_hbm.at[page_id], rope_buf.at[sl, i], rope_sem.at[sl]))
        return copies

    @pl.when(b == 0)
    def _():
        for c in page_copies(0, 0):
            c.start()

    @pl.when(b + 1 < n_samples)
    def _():
        for c in page_copies(b + 1, 1 - slot):
            c.start()

    for c in page_copies(b, slot):
        c.wait()

    qlat = qlat_ref[b]
    qpe = qpe_ref[b]
    own_ckv = ckvs_ref[pl.ds(b, 1), :].astype(BF16).astype(F32)
    own_kpe = kpes_ref[pl.ds(b, 1), :].astype(BF16).astype(F32)
    s_own = (jnp.sum(qlat.astype(F32) * own_ckv, axis=1, keepdims=True)
             + jnp.sum(qpe.astype(F32) * own_kpe, axis=1, keepdims=True)) * sm_scale

    chunk = DECODE_CHUNK_PAGES
    ckv_parts, s_parts = [], []
    for c0 in range(0, n_pages, chunk):
        ckv = jnp.concatenate([kv_buf[slot, i] for i in range(c0, c0 + chunk)], axis=0).astype(BF16)
        kpe_t = jnp.concatenate([rope_buf[slot, i] for i in range(c0, c0 + chunk)], axis=1).astype(BF16)
        ckv_parts.append(ckv)
        s_parts.append((_dot_nt(qlat, ckv) + _dot(qpe, kpe_t)) * sm_scale)
    m = s_own
    for s in s_parts:
        m = jnp.maximum(m, jnp.max(s, axis=1, keepdims=True))
    p_own = jnp.exp(s_own - m)
    l = p_own
    acc = p_own.astype(BF16).astype(F32) * own_ckv
    for s, ckv in zip(s_parts, ckv_parts):
        p = jnp.exp(s - m)
        l = l + jnp.sum(p, axis=1, keepdims=True)
        acc = acc + _dot(p.astype(BF16), ckv)
    o_ref[b] = acc / l


def _decode(page_table, q_lat, q_pe, ckv_own, kpe_own, cache_kv, cache_rope_t, *, sm_scale):
    bs, n_heads, kv_lora = q_lat.shape
    n_pages = page_table.shape[1]
    page = cache_kv.shape[1]
    assert n_pages % DECODE_CHUNK_PAGES == 0
    whole = lambda a: pl.BlockSpec(a.shape, lambda b, pt, nd=a.ndim: (0,) * nd)
    grid_spec = pltpu.PrefetchScalarGridSpec(
        num_scalar_prefetch=1,
        grid=(bs,),
        in_specs=[whole(q_lat), whole(q_pe), whole(ckv_own), whole(kpe_own),
                  pl.BlockSpec(memory_space=pl.ANY), pl.BlockSpec(memory_space=pl.ANY)],
        out_specs=pl.BlockSpec((bs, n_heads, kv_lora), lambda b, pt: (0, 0, 0)),
        scratch_shapes=[
            pltpu.VMEM((2, n_pages, page, kv_lora), F32),
            pltpu.VMEM((2, n_pages, QK_ROPE, page), F32),
            pltpu.SemaphoreType.DMA((2,)),
            pltpu.SemaphoreType.DMA((2,)),
        ],
    )
    return pl.pallas_call(
        functools.partial(_decode_kernel, n_pages=n_pages, sm_scale=sm_scale),
        out_shape=jax.ShapeDtypeStruct((bs, n_heads, kv_lora), F32),
        grid_spec=grid_spec,
        compiler_params=_params(1),
        name="decode",
    )(page_table.reshape(-1), q_lat, q_pe, ckv_own, kpe_own, cache_kv, cache_rope_t)


def _sample_mixers(olat_ref, wuv_ref, sp_ref, us_ref, wp_ref, scale_ref, *, past_len):
    n_heads = olat_ref.shape[0]
    group = us_ref.shape[1] // len(POOL_WINDOWS)
    attn = jnp.concatenate(
        [_dot(olat_ref[h].astype(BF16), wuv_ref[h]).astype(BF16) for h in range(n_heads)], axis=1)
    pooled = []
    for g, w in enumerate(POOL_WINDOWS):
        cols = slice(g * group, (g + 1) * group)
        cur = us_ref[:, cols]
        acc = cur
        for k in range(1, w):
            acc = acc + sp_ref[POOL_STATE - k, :, cols]
        delta = _pool_delta(acc, cur, 1.0 / min(past_len + 1, w))
        out = _dot(delta.astype(BF16), wp_ref[g])
        pooled.append((out * scale_ref[:, cols]).astype(BF16))
    return jnp.concatenate(pooled, axis=1), attn


def _outproj_kernel(hp_ref, poolp_ref, attnp_ref, hs_ref, olat_ref, wuv_ref, sp_ref, us_ref, wp_ref,
                    scale_ref, wo_ref, g_ref, yp_ref, ys_ref, *, n_prompt_tiles, d_pool, past_len):
    i = pl.program_id(0)

    def body(h_ref, pool, attn, y_ref):
        mix = _dot(pool, wo_ref[:d_pool, :]) + _dot(attn, wo_ref[d_pool:, :])
        y_ref[...] = h_ref[...] + _rms(mix, g_ref[...])

    @pl.when(i < n_prompt_tiles)
    def _():
        heads = [attnp_ref[h] for h in range(attnp_ref.shape[0])]
        body(hp_ref, poolp_ref[...], jnp.concatenate(heads, axis=1), yp_ref)

    @pl.when(i == n_prompt_tiles)
    def _():
        pool, attn = _sample_mixers(olat_ref, wuv_ref, sp_ref, us_ref, wp_ref, scale_ref, past_len=past_len)
        body(hs_ref, pool, attn, ys_ref)


def _outproj(hp, pool_p, attn_p, hs, o_lat, w_uv_h, state_t, u_s, w_pool, pool_scale, w_out, g_post,
             *, past_len):
    mp, d = hp.shape
    ms = hs.shape[0]
    d_pool = pool_p.shape[1]
    n_heads, _, v_head = attn_p.shape
    tm = ROW_TILE
    ntp = mp // tm
    prompt_row = lambda i: (jnp.minimum(i, ntp - 1), 0)
    whole = lambda a: pl.BlockSpec(a.shape, lambda i, nd=a.ndim: (0,) * nd)
    return pl.pallas_call(
        functools.partial(_outproj_kernel, n_prompt_tiles=ntp, d_pool=d_pool, past_len=past_len),
        out_shape=(jax.ShapeDtypeStruct((mp, d), F32), jax.ShapeDtypeStruct((ms, d), F32)),
        grid=(ntp + 1,),
        in_specs=[
            pl.BlockSpec((tm, d), prompt_row),
            pl.BlockSpec((tm, d_pool), prompt_row),
            pl.BlockSpec((n_heads, tm, v_head), lambda i: (0, jnp.minimum(i, ntp - 1), 0)),
            whole(hs), whole(o_lat), whole(w_uv_h), whole(state_t), whole(u_s), whole(w_pool),
            whole(pool_scale), whole(w_out), whole(g_post),
        ],
        out_specs=(pl.BlockSpec((tm, d), prompt_row), pl.BlockSpec((ms, d), lambda i: (0, 0))),
        compiler_params=_params(1),
        name="outproj",
    )(hp, pool_p, attn_p, hs, o_lat, w_uv_h, state_t, u_s, w_pool, pool_scale, w_out, g_post)


def _rotate_half_columns(w):
    half = QK_ROPE // 2
    return jnp.concatenate([-w[..., half:], w[..., :half]], axis=-1)


def _rope_table(pos):
    half = QK_ROPE // 2
    inv = ROPE_THETA ** (-jnp.arange(half, dtype=F32) / half)
    ang = pos.astype(F32)[:, None] * inv[None, :]
    cos, sin = jnp.cos(ang), jnp.sin(ang)
    return jnp.concatenate([cos, cos, sin, sin], axis=-1)


def kernel(x_prompt, x_sample, cache_kv_latent, cache_k_rope, state_pool, page_table,
           g_ffn1_pre, w1_gate, w1_up, w1_down, g_ffn1_post,
           g_mix_pre, w_in, w_pool, pool_scale, g_q, w_uq, g_kv, w_uk, w_uv, w_out, g_mix_post,
           g_ffn2_pre, w2_gate, w2_up, w2_down, g_ffn2_post):
    depth = w_in.shape[0]
    bp, sp, d = x_prompt.shape
    bs, ts, _ = x_sample.shape
    n_pages = page_table.shape[1]
    page = cache_kv_latent.shape[2]
    past_len = n_pages * page
    kv_lora, n_heads, _ = w_uk.shape[1:]
    q_lora = w_uq.shape[1]
    d_pool = w_pool.shape[1] * w_pool.shape[2]
    assert ts == 1, "the decode kernel handles one new token per sample sequence"
    sm_scale = (QK_NOPE + QK_ROPE) ** -0.5

    cs_p = _rope_table(jnp.arange(sp, dtype=jnp.int32))
    cs_s = jnp.tile(_rope_table(past_len + jnp.arange(ts, dtype=jnp.int32)), (bs, 1))

    hp = x_prompt.reshape(bp * sp, d)
    hs = x_sample.reshape(bs * ts, d)
    outs = [[] for _ in range(6)]
    for l in range(depth):
        row = lambda g: g[l][None, :]
        w_in_t = jnp.swapaxes(w_in[l], 0, 1)
        w_rope_t = jnp.swapaxes(_rotate_half_columns(w_in[l][:, d_pool + q_lora + kv_lora:]), 0, 1)
        w_in2 = jnp.concatenate([w_in_t, w_rope_t], axis=0).astype(BF16)
        wq = w_uq[l].reshape(q_lora, n_heads, QK_NOPE + QK_ROPE)
        w_q2 = jnp.concatenate(
            [wq, _rotate_half_columns(wq[..., QK_NOPE:])], axis=-1).reshape(q_lora, n_heads * HEAD_W).astype(BF16)
        w_kv2 = jnp.concatenate(
            [w_uk[l].reshape(kv_lora, -1), w_uv[l].reshape(kv_lora, -1)], axis=1).astype(BF16)
        w_uk_t = jnp.transpose(w_uk[l], (1, 2, 0)).astype(BF16)
        w_uv_h = jnp.transpose(w_uv[l], (1, 0, 2)).astype(BF16)
        w_pool_l = w_pool[l].astype(BF16)
        w_out_l = w_out[l].astype(BF16)

        hp, hs = _ffn(hp, hs, row(g_ffn1_pre), w1_gate[l], w1_up[l], w1_down[l], row(g_ffn1_post))
        (u_p, ckv_p, kpe_p, q_p, k_p, v_p, pool_p, u_s, ckv_s, kpe_s, qlat_s, qpe_s) = _proj(
            hp, hs, cs_p, cs_s, row(g_mix_pre), w_in2, row(g_q), w_q2, row(g_kv), w_kv2, w_uk_t,
            w_pool_l, row(pool_scale), seq=sp, n_heads=n_heads)
        attn_p = _attn(q_p, k_p, v_p, batch=bp, seq=sp, sm_scale=sm_scale)
        o_lat = _decode(page_table, jnp.transpose(qlat_s, (1, 0, 2)), jnp.transpose(qpe_s, (1, 0, 2)),
                        ckv_s, kpe_s, cache_kv_latent[l],
                        jnp.swapaxes(cache_k_rope[l], 1, 2),
                        sm_scale=sm_scale)
        state_t = jnp.transpose(state_pool[l], (1, 0, 2))
        hp, hs = _outproj(hp, pool_p, attn_p, hs, jnp.transpose(o_lat, (1, 0, 2)), w_uv_h, state_t, u_s,
                          w_pool_l, row(pool_scale), w_out_l, row(g_mix_post), past_len=past_len)
        hp, hs = _ffn(hp, hs, row(g_ffn2_pre), w2_gate[l], w2_up[l], w2_down[l], row(g_ffn2_post))

        u_p3 = u_p.reshape(bp, sp, d_pool)
        outs[0].append(ckv_p.reshape(bp, sp, kv_lora))
        outs[1].append(jnp.swapaxes(kpe_p, 1, 2))
        outs[2].append(u_p3[:, sp - POOL_STATE:])
        outs[3].append(ckv_s.reshape(bs, ts, kv_lora))
        outs[4].append(kpe_s.reshape(bs, ts, QK_ROPE))
        new_rows_t = jnp.transpose(u_s.reshape(bs, ts, d_pool), (1, 0, 2))
        outs[5].append(jnp.transpose(jnp.concatenate([state_t[ts:], new_rows_t], axis=0), (1, 0, 2)))
    return (hp.reshape(bp, sp, d), hs.reshape(bs, ts, d), *[jnp.stack(o) for o in outs])
```

```python
import functools
import math

import jax
import jax.numpy as jnp
from jax import lax
from jax.experimental import pallas as pl
from jax.experimental.pallas import tpu as pltpu

F32 = jnp.float32
BF16 = jnp.bfloat16

EPS = 1e-6
ROPE_THETA = 10000.0
POOL_WINDOWS = (2, 4, 8, 16)
POOL_STATE = max(POOL_WINDOWS) - 1
QK_NOPE = 128
QK_ROPE = 64
V_HEAD = 128
HEAD_W = 2 * 128

V7X_VMEM_BYTES = 64 * 1024 * 1024
VMEM_LIMIT = V7X_VMEM_BYTES - 8 * 1024 * 1024

ROW_TILE = 512
FFN_ROW_TILE = 1024
FF_TILE = 256
NORM_ROWS = 128
PROJ_TILE = 512
ATTN_TILE = 512
ATTN_HEADS_IN_FLIGHT = 4
DECODE_CHUNK_PAGES = 32


def _params(n_grid_dims):
    return pltpu.CompilerParams(
        dimension_semantics=("arbitrary",) * n_grid_dims, vmem_limit_bytes=VMEM_LIMIT)


def _rms(x, g):
    return x * lax.rsqrt(jnp.mean(x * x, axis=-1, keepdims=True) + EPS) * g


def _dot(a, b):
    return jnp.dot(a, b, preferred_element_type=F32)


def _dot_nt(a, b):
    return lax.dot_general(a, b, (((1,), (1,)), ((), ())), preferred_element_type=F32)


def _for_row_chunks(rows, fn):
    n_chunks, rem = divmod(rows, NORM_ROWS)
    assert rem == 0

    def step(c, carry):
        fn(pl.ds(pl.multiple_of(c * NORM_ROWS, NORM_ROWS), NORM_ROWS))
        return carry

    lax.fori_loop(0, n_chunks, step, 0, unroll=2 if n_chunks % 2 == 0 else 1)


def _row_inv_rms(x):
    return lax.rsqrt(jnp.mean(x * x, axis=-1, keepdims=True) + EPS)


def _ffn_kernel(xp_ref, xs_ref, gpre_ref, wg_hbm, wu_hbm, wd_hbm, gpost_ref,
                yp_ref, ys_ref, xnp_ref, xns_ref, wg_buf, wu_buf, wd_buf, w_sem, *, n_ff_tiles):
    row_tile = pl.program_id(0)
    last_tile = row_tile == pl.num_programs(0) - 1
    tm = xp_ref.shape[0]
    n_pairs = n_ff_tiles // 2

    def weight_copies(j, slot):
        cols = pl.ds(pl.multiple_of(j * FF_TILE, FF_TILE), FF_TILE)
        return (pltpu.make_async_copy(wg_hbm.at[:, cols], wg_buf.at[slot], w_sem.at[0, slot]),
                pltpu.make_async_copy(wu_hbm.at[:, cols], wu_buf.at[slot], w_sem.at[1, slot]),
                pltpu.make_async_copy(wd_hbm.at[cols, :], wd_buf.at[slot], w_sem.at[2, slot]))

    def start(j, slot):
        for c in weight_copies(j, slot):
            c.start()

    def wait(j, slot):
        for c in weight_copies(j, slot):
            c.wait()

    def prologue(x_ref, y_ref, xn_ref):
        def chunk(rs):
            inv = _row_inv_rms(x_ref[rs, :])
            xn_ref[rs, :] = (x_ref[rs, :] * inv * gpre_ref[...]).astype(BF16)
            y_ref[rs, :] = jnp.zeros((NORM_ROWS, y_ref.shape[1]), F32)

        _for_row_chunks(x_ref.shape[0], chunk)

    def epilogue(x_ref, y_ref):
        half_gain = 0.5 * gpost_ref[...]

        def chunk(rs):
            inv = _row_inv_rms(y_ref[rs, :])
            y_ref[rs, :] = x_ref[rs, :] + y_ref[rs, :] * inv * half_gain

        _for_row_chunks(x_ref.shape[0], chunk)

    def swiglu_part(xn, slot):
        gate = _dot(xn, wg_buf[slot].astype(BF16))
        up = _dot(xn, wu_buf[slot].astype(BF16))
        hidden = (gate / (1.0 + jnp.exp(-gate))) * up
        return _dot(hidden.astype(BF16), wd_buf[slot].astype(BF16))

    def accumulate(slot, with_sample):
        if with_sample:
            part = swiglu_part(jnp.concatenate([xnp_ref[...], xns_ref[...]], axis=0), slot)
            yp_ref[...] += part[:tm]
            ys_ref[...] += part[tm:]
        else:
            yp_ref[...] += swiglu_part(xnp_ref[...], slot)

    def ff_loop(with_sample):
        def pair(jj, carry):
            j = 2 * jj
            start(j + 1, 1)
            wait(j, 0)
            accumulate(0, with_sample)
            if with_sample:
                pl.when(jj < n_pairs - 1)(lambda: start(j + 2, 0))
            else:
                start(lax.rem(j + 2, n_ff_tiles), 0)
            wait(j + 1, 1)
            accumulate(1, with_sample)
            return carry

        lax.fori_loop(0, n_pairs, pair, 0)

    pl.when(row_tile == 0)(lambda: start(0, 0))
    prologue(xp_ref, yp_ref, xnp_ref)
    pl.when(last_tile)(lambda: prologue(xs_ref, ys_ref, xns_ref))
    pl.when(jnp.logical_not(last_tile))(lambda: ff_loop(False))
    pl.when(last_tile)(lambda: ff_loop(True))
    epilogue(xp_ref, yp_ref)
    pl.when(last_tile)(lambda: epilogue(xs_ref, ys_ref))


def _ffn(xp, xs, g_pre, w_gate, w_up, w_down, g_post):
    mp, d = xp.shape
    ms = xs.shape[0]
    d_ff = w_gate.shape[1]
    tm = FFN_ROW_TILE
    nj = d_ff // FF_TILE
    assert mp % tm == 0 and d_ff % FF_TILE == 0 and nj % 2 == 0
    prompt_row = lambda i: (i, 0)
    fixed = lambda i: (0, 0)
    in_hbm = pl.BlockSpec(memory_space=pl.ANY)
    return pl.pallas_call(
        functools.partial(_ffn_kernel, n_ff_tiles=nj),
        out_shape=(jax.ShapeDtypeStruct((mp, d), F32), jax.ShapeDtypeStruct((ms, d), F32)),
        grid=(mp // tm,),
        in_specs=[
            pl.BlockSpec((tm, d), prompt_row),
            pl.BlockSpec((ms, d), fixed),
            pl.BlockSpec((1, d), fixed),
            in_hbm, in_hbm, in_hbm,
            pl.BlockSpec((1, d), fixed),
        ],
        out_specs=(pl.BlockSpec((tm, d), prompt_row), pl.BlockSpec((ms, d), fixed)),
        scratch_shapes=[
            pltpu.VMEM((tm, d), BF16), pltpu.VMEM((ms, d), BF16),
            pltpu.VMEM((2, d, FF_TILE), F32), pltpu.VMEM((2, d, FF_TILE), F32),
            pltpu.VMEM((2, FF_TILE, d), F32),
            pltpu.SemaphoreType.DMA((3, 2)),
        ],
        compiler_params=_params(1),
        name="ffn",
    )(xp, xs, g_pre, w_gate, w_up, w_down, g_post)


def _rope_pairs(slab, cs):
    t = slab * cs
    return t + pltpu.roll(t, QK_ROPE, axis=1)


def _proj_kernel(hp_ref, hs_ref, csp_ref, css_ref, gmix_ref, win_ref, gq_ref, wq_ref, gkv_ref,
                 wkv_ref, wukt_ref, wp_ref, pscale_ref,
                 up_ref, ckvp_ref, kpep_ref, q_ref, k_ref, v_ref, poolp_ref,
                 us_ref, ckvs_ref, kpes_ref, qlat_ref, qpes_ref, ext_ref,
                 *, n_prompt_tiles, tiles_per_seq, n_heads, d_pool, q_lora, kv_lora):
    i = pl.program_id(0)

    @pl.when(i == 0)
    def _():
        ext_ref[...] = jnp.zeros(ext_ref.shape, F32)

    def common(h_ref, cs_ref, u_ref, ckv_ref):
        hn = _rms(h_ref[...], gmix_ref[...]).astype(BF16)
        z = _dot_nt(hn, win_ref[...])
        u = z[:, :d_pool]
        u_ref[...] = u
        cqn = _rms(z[:, d_pool:d_pool + q_lora], gq_ref[...]).astype(BF16)
        ckv = _rms(z[:, d_pool + q_lora:d_pool + q_lora + kv_lora], gkv_ref[...])
        ckv_ref[...] = ckv
        cs = cs_ref[...]
        kpe = _rope_pairs(z[:, d_pool + q_lora + kv_lora:], cs)
        qq = _dot(cqn, wq_ref[...])
        return u, ckv, kpe, qq, cs

    @pl.when(i < n_prompt_tiles)
    def _():
        u, ckv, kpe, qq, cs = common(hp_ref, csp_ref, up_ref, ckvp_ref)
        kpep_ref[0] = kpe.T[:QK_ROPE, :]
        _pool_rows(u, lax.rem(i, tiles_per_seq), wp_ref, pscale_ref, poolp_ref, ext_ref)
        lane = lax.broadcasted_iota(jnp.int32, kpe.shape, 1)
        kpe_pad = jnp.where(lane < QK_ROPE, kpe, 0.0).astype(BF16)
        kv = _dot(ckv.astype(BF16), wkv_ref[...])
        for h in range(n_heads):
            v_lo = n_heads * QK_NOPE + h * V_HEAD
            v_ref[h] = kv[:, v_lo:v_lo + V_HEAD].astype(BF16)
            slab = qq[:, h * HEAD_W:(h + 1) * HEAD_W]
            q_ref[h, :, :QK_NOPE] = slab[:, :QK_NOPE].astype(BF16)
            q_ref[h, :, QK_NOPE:] = _rope_pairs(slab[:, QK_NOPE:], cs).astype(BF16)
            k_ref[h, :, :QK_NOPE] = kv[:, h * QK_NOPE:(h + 1) * QK_NOPE].astype(BF16)
            k_ref[h, :, QK_NOPE:] = kpe_pad

    @pl.when(i == n_prompt_tiles)
    def _():
        _, _, kpe, qq, cs = common(hs_ref, css_ref, us_ref, ckvs_ref)
        kpes_ref[...] = kpe[:, :QK_ROPE]
        for h in range(n_heads):
            slab = qq[:, h * HEAD_W:(h + 1) * HEAD_W]
            qlat_ref[h] = _dot(slab[:, :QK_NOPE].astype(BF16), wukt_ref[h]).astype(BF16)
            qpes_ref[h] = _rope_pairs(slab[:, QK_NOPE:], cs)[:, :QK_ROPE].astype(BF16)


def _proj(hp, hs, cs_p, cs_s, g_mix, w_in2, g_q, w_q2, g_kv, w_kv2, w_uk_t, w_pool, pool_scale, *, seq, n_heads):
    mp, d = hp.shape
    ms = hs.shape[0]
    tm = PROJ_TILE
    ntp = mp // tm
    tiles_per_seq = seq // tm
    assert mp % tm == 0 and seq % tm == 0
    d_in2 = w_in2.shape[0]
    q_lora = w_q2.shape[0]
    kv_lora = w_kv2.shape[0]
    d_pool = d_in2 - q_lora - kv_lora - 2 * QK_ROPE
    prompt_row = lambda i: (jnp.minimum(i, ntp - 1), 0)
    prompt_head_row = lambda i: (0, jnp.minimum(i, ntp - 1), 0)
    fixed2 = lambda i: (0, 0)
    fixed3 = lambda i: (0, 0, 0)
    whole = lambda a: pl.BlockSpec(a.shape, fixed2 if a.ndim == 2 else fixed3)
    out_shape = (
        jax.ShapeDtypeStruct((mp, d_pool), F32),
        jax.ShapeDtypeStruct((mp, kv_lora), F32),
        jax.ShapeDtypeStruct((mp // seq, QK_ROPE, seq), F32),
        jax.ShapeDtypeStruct((n_heads, mp, HEAD_W), BF16),
        jax.ShapeDtypeStruct((n_heads, mp, HEAD_W), BF16),
        jax.ShapeDtypeStruct((n_heads, mp, V_HEAD), BF16),
        jax.ShapeDtypeStruct((mp, d_pool), BF16),
        jax.ShapeDtypeStruct((ms, d_pool), F32),
        jax.ShapeDtypeStruct((ms, kv_lora), F32),
        jax.ShapeDtypeStruct((ms, QK_ROPE), F32),
        jax.ShapeDtypeStruct((n_heads, ms, kv_lora), BF16),
        jax.ShapeDtypeStruct((n_heads, ms, QK_ROPE), BF16),
    )
    out_specs = (
        pl.BlockSpec((tm, d_pool), prompt_row),
        pl.BlockSpec((tm, kv_lora), prompt_row),
        pl.BlockSpec((1, QK_ROPE, tm), lambda i: (jnp.minimum(i, ntp - 1) // tiles_per_seq, 0,
                                                  jnp.minimum(i, ntp - 1) % tiles_per_seq)),
        pl.BlockSpec((n_heads, tm, HEAD_W), prompt_head_row),
        pl.BlockSpec((n_heads, tm, HEAD_W), prompt_head_row),
        pl.BlockSpec((n_heads, tm, V_HEAD), prompt_head_row),
        pl.BlockSpec((tm, d_pool), prompt_row),
        pl.BlockSpec((ms, d_pool), fixed2),
        pl.BlockSpec((ms, kv_lora), fixed2),
        pl.BlockSpec((ms, QK_ROPE), fixed2),
        pl.BlockSpec((n_heads, ms, kv_lora), fixed3),
        pl.BlockSpec((n_heads, ms, QK_ROPE), fixed3),
    )
    return pl.pallas_call(
        functools.partial(_proj_kernel, n_prompt_tiles=ntp, tiles_per_seq=tiles_per_seq, n_heads=n_heads,
                          d_pool=d_pool, q_lora=q_lora, kv_lora=kv_lora),
        out_shape=out_shape,
        grid=(ntp + 1,),
        in_specs=[
            pl.BlockSpec((tm, d), prompt_row),
            pl.BlockSpec((ms, d), fixed2),
            pl.BlockSpec((tm, 2 * QK_ROPE), lambda i: (jnp.minimum(i, ntp - 1) % tiles_per_seq, 0)),
            whole(cs_s), whole(g_mix), whole(w_in2), whole(g_q), whole(w_q2), whole(g_kv),
            whole(w_kv2), whole(w_uk_t), whole(w_pool), whole(pool_scale),
        ],
        out_specs=out_specs,
        scratch_shapes=[pltpu.VMEM((tm + POOL_STATE + 1, d_pool), F32)],
        compiler_params=_params(1),
        name="proj",
    )(hp, hs, cs_p, cs_s, g_mix, w_in2, g_q, w_q2, g_kv, w_kv2, w_uk_t, w_pool, pool_scale)


def _pool_delta(window_sum, cur, inv_cnt):
    return window_sum * inv_cnt - cur


def _pool_rows(u, seq_tile, wp_ref, scale_ref, o_ref, ext_ref):
    rows, d_pool = u.shape
    group = d_pool // len(POOL_WINDOWS)
    halo = POOL_STATE + 1

    ext_ref[:halo, :] = jnp.where(seq_tile == 0, 0.0, ext_ref[rows:rows + halo, :])
    ext_ref[halo:, :] = u
    pos = seq_tile * rows + lax.broadcasted_iota(jnp.int32, (rows, 1), 0)
    for g, w in enumerate(POOL_WINDOWS):
        cols = slice(g * group, (g + 1) * group)
        cur = ext_ref[halo:, cols]
        acc = cur
        for k in range(1, w):
            acc = acc + ext_ref[halo - k:halo - k + rows, cols]
        inv_cnt = 1.0 / jnp.minimum(pos + 1, w).astype(F32)
        delta = _pool_delta(acc, cur, inv_cnt)
        out = _dot(delta.astype(BF16), wp_ref[g])
        o_ref[:, cols] = (out * scale_ref[:, cols]).astype(BF16)


def _attn_kernel(q_ref, k_ref, v_ref, o_ref, *, n_heads, n_q_tiles, exp2_scale):
    qi = pl.program_id(1)
    t = ATTN_TILE
    row = lax.broadcasted_iota(jnp.int32, (t, t), 0)
    col = lax.broadcasted_iota(jnp.int32, (t, t), 1)
    causal = row >= col

    def tile(n_past):
        def head(h, carry):
            q = q_ref[h]
            s_diag = jnp.where(causal, _dot_nt(q, k_ref[h, n_past:n_past + t, :]), -jnp.inf)
            m = jnp.max(s_diag, axis=1, keepdims=True)
            if n_past:
                s_past = _dot_nt(q, k_ref[h, :n_past, :])
                m = jnp.maximum(m, jnp.max(s_past, axis=1, keepdims=True))
            p_diag = jnp.exp2((s_diag - m) * exp2_scale)
            l = jnp.sum(p_diag, axis=1, keepdims=True)
            o = _dot(p_diag.astype(BF16), v_ref[h, n_past:n_past + t, :])
            if n_past:
                p_past = jnp.exp2((s_past - m) * exp2_scale)
                l = l + jnp.sum(p_past, axis=1, keepdims=True)
                o = o + _dot(p_past.astype(BF16), v_ref[h, :n_past, :])
            o_ref[h] = (o / l).astype(BF16)
            return carry

        lax.fori_loop(0, n_heads, head, 0, unroll=ATTN_HEADS_IN_FLIGHT)

    for kq in range(n_q_tiles):
        pl.when(qi == kq)(functools.partial(tile, kq * t))


def _attn(q, k, v, *, batch, seq, sm_scale):
    n_heads, m, _ = q.shape
    t = ATTN_TILE
    nq = seq // t
    assert seq % t == 0
    return pl.pallas_call(
        functools.partial(_attn_kernel, n_heads=n_heads, n_q_tiles=nq,
                          exp2_scale=sm_scale * math.log2(math.e)),
        out_shape=jax.ShapeDtypeStruct((n_heads, m, V_HEAD), BF16),
        grid=(batch, nq),
        in_specs=[
            pl.BlockSpec((n_heads, t, HEAD_W), lambda b, i: (0, b * nq + i, 0)),
            pl.BlockSpec((n_heads, seq, HEAD_W), lambda b, i: (0, b, 0)),
            pl.BlockSpec((n_heads, seq, V_HEAD), lambda b, i: (0, b, 0)),
        ],
        out_specs=pl.BlockSpec((n_heads, t, V_HEAD), lambda b, i: (0, b * nq + i, 0)),
        compiler_params=_params(2),
        name="attn",
    )(q, k, v)


def _decode_kernel(pt_ref, qlat_ref, qpe_ref, ckvs_ref, kpes_ref, kv_hbm, rope_hbm, o_ref,
                   kv_buf, rope_buf, kv_sem, rope_sem, *, n_pages, sm_scale):
    b = pl.program_id(0)
    n_samples = pl.num_programs(0)
    slot = b % 2

    def page_copies(sample, sl):
        copies = []
        for i in range(n_pages):
            page_id = pt_ref[sample * n_pages + i]
            copies.append(pltpu.make_async_copy(kv_hbm.at[page_id], kv_buf.at[sl, i], kv_sem.at[sl]))
            copies.append(pltpu.make_async_copy(rope_hbm.at[page_id], rope_buf.at[sl, i], rope_sem.at[sl]))
        return copies

    @pl.when(b == 0)
    def _():
        for c in page_copies(0, 0):
            c.start()

    @pl.when(b + 1 < n_samples)
    def _():
        for c in page_copies(b + 1, 1 - slot):
            c.start()

    for c in page_copies(b, slot):
        c.wait()

    qlat = qlat_ref[b]
    qpe = qpe_ref[b]
    own_ckv = ckvs_ref[pl.ds(b, 1), :].astype(BF16).astype(F32)
    own_kpe = kpes_ref[pl.ds(b, 1), :].astype(BF16).astype(F32)
    s_own = (jnp.sum(qlat.astype(F32) * own_ckv, axis=1, keepdims=True)
             + jnp.sum(qpe.astype(F32) * own_kpe, axis=1, keepdims=True)) * sm_scale

    chunk = DECODE_CHUNK_PAGES
    ckv_parts, s_parts = [], []
    for c0 in range(0, n_pages, chunk):
        ckv = jnp.concatenate([kv_buf[slot, i] for i in range(c0, c0 + chunk)], axis=0).astype(BF16)
        kpe_t = jnp.concatenate([rope_buf[slot, i] for i in range(c0, c0 + chunk)], axis=1).astype(BF16)
        ckv_parts.append(ckv)
        s_parts.append((_dot_nt(qlat, ckv) + _dot(qpe, kpe_t)) * sm_scale)
    m = s_own
    for s in s_parts:
        m = jnp.maximum(m, jnp.max(s, axis=1, keepdims=True))
    p_own = jnp.exp(s_own - m)
    l = p_own
    acc = p_own.astype(BF16).astype(F32) * own_ckv
    for s, ckv in zip(s_parts, ckv_parts):
        p = jnp.exp(s - m)
        l = l + jnp.sum(p, axis=1, keepdims=True)
        acc = acc + _dot(p.astype(BF16), ckv)
    o_ref[b] = acc / l


def _decode(page_table, q_lat, q_pe, ckv_own, kpe_own, cache_kv, cache_rope_t, *, sm_scale):
    bs, n_heads, kv_lora = q_lat.shape
    n_pages = page_table.shape[1]
    page = cache_kv.shape[1]
    assert n_pages % DECODE_CHUNK_PAGES == 0
    whole = lambda a: pl.BlockSpec(a.shape, lambda b, pt, nd=a.ndim: (0,) * nd)
    grid_spec = pltpu.PrefetchScalarGridSpec(
        num_scalar_prefetch=1,
        grid=(bs,),
        in_specs=[whole(q_lat), whole(q_pe), whole(ckv_own), whole(kpe_own),
                  pl.BlockSpec(memory_space=pl.ANY), pl.BlockSpec(memory_space=pl.ANY)],
        out_specs=pl.BlockSpec((bs, n_heads, kv_lora), lambda b, pt: (0, 0, 0)),
        scratch_shapes=[
            pltpu.VMEM((2, n_pages, page, kv_lora), F32),
            pltpu.VMEM((2, n_pages, QK_ROPE, page), F32),
            pltpu.SemaphoreType.DMA((2,)),
            pltpu.SemaphoreType.DMA((2,)),
        ],
    )
    return pl.pallas_call(
        functools.partial(_decode_kernel, n_pages=n_pages, sm_scale=sm_scale),
        out_shape=jax.ShapeDtypeStruct((bs, n_heads, kv_lora), F32),
        grid_spec=grid_spec,
        compiler_params=_params(1),
        name="decode",
    )(page_table.reshape(-1), q_lat, q_pe, ckv_own, kpe_own, cache_kv, cache_rope_t)


def _sample_mixers(olat_ref, wuv_ref, sp_ref, us_ref, wp_ref, scale_ref, *, past_len):
    n_heads = olat_ref.shape[0]
    group = us_ref.shape[1] // len(POOL_WINDOWS)
    attn = jnp.concatenate(
        [_dot(olat_ref[h].astype(BF16), wuv_ref[h]).astype(BF16) for h in range(n_heads)], axis=1)
    pooled = []
    for g, w in enumerate(POOL_WINDOWS):
        cols = slice(g * group, (g + 1) * group)
        cur = us_ref[:, cols]
        acc = cur
        for k in range(1, w):
            acc = acc + sp_ref[POOL_STATE - k, :, cols]
        delta = _pool_delta(acc, cur, 1.0 / min(past_len + 1, w))
        out = _dot(delta.astype(BF16), wp_ref[g])
        pooled.append((out * scale_ref[:, cols]).astype(BF16))
    return jnp.concatenate(pooled, axis=1), attn


def _outproj_kernel(hp_ref, poolp_ref, attnp_ref, hs_ref, olat_ref, wuv_ref, sp_ref, us_ref, wp_ref,
                    scale_ref, wo_ref, g_ref, yp_ref, ys_ref, *, n_prompt_tiles, d_pool, past_len):
    i = pl.program_id(0)

    def body(h_ref, pool, attn, y_ref):
        mix = _dot(pool, wo_ref[:d_pool, :]) + _dot(attn, wo_ref[d_pool:, :])
        y_ref[...] = h_ref[...] + _rms(mix, g_ref[...])

    @pl.when(i < n_prompt_tiles)
    def _():
        heads = [attnp_ref[h] for h in range(attnp_ref.shape[0])]
        body(hp_ref, poolp_ref[...], jnp.concatenate(heads, axis=1), yp_ref)

    @pl.when(i == n_prompt_tiles)
    def _():
        pool, attn = _sample_mixers(olat_ref, wuv_ref, sp_ref, us_ref, wp_ref, scale_ref, past_len=past_len)
        body(hs_ref, pool, attn, ys_ref)


def _outproj(hp, pool_p, attn_p, hs, o_lat, w_uv_h, state_t, u_s, w_pool, pool_scale, w_out, g_post,
             *, past_len):
    mp, d = hp.shape
    ms = hs.shape[0]
    d_pool = pool_p.shape[1]
    n_heads, _, v_head = attn_p.shape
    tm = ROW_TILE
    ntp = mp // tm
    prompt_row = lambda i: (jnp.minimum(i, ntp - 1), 0)
    whole = lambda a: pl.BlockSpec(a.shape, lambda i, nd=a.ndim: (0,) * nd)
    return pl.pallas_call(
        functools.partial(_outproj_kernel, n_prompt_tiles=ntp, d_pool=d_pool, past_len=past_len),
        out_shape=(jax.ShapeDtypeStruct((mp, d), F32), jax.ShapeDtypeStruct((ms, d), F32)),
        grid=(ntp + 1,),
        in_specs=[
            pl.BlockSpec((tm, d), prompt_row),
            pl.BlockSpec((tm, d_pool), prompt_row),
            pl.BlockSpec((n_heads, tm, v_head), lambda i: (0, jnp.minimum(i, ntp - 1), 0)),
            whole(hs), whole(o_lat), whole(w_uv_h), whole(state_t), whole(u_s), whole(w_pool),
            whole(pool_scale), whole(w_out), whole(g_post),
        ],
        out_specs=(pl.BlockSpec((tm, d), prompt_row), pl.BlockSpec((ms, d), lambda i: (0, 0))),
        compiler_params=_params(1),
        name="outproj",
    )(hp, pool_p, attn_p, hs, o_lat, w_uv_h, state_t, u_s, w_pool, pool_scale, w_out, g_post)


def _rotate_half_columns(w):
    half = QK_ROPE // 2
    return jnp.concatenate([-w[..., half:], w[..., :half]], axis=-1)


def _rope_table(pos):
    half = QK_ROPE // 2
    inv = ROPE_THETA ** (-jnp.arange(half, dtype=F32) / half)
    ang = pos.astype(F32)[:, None] * inv[None, :]
    cos, sin = jnp.cos(ang), jnp.sin(ang)
    return jnp.concatenate([cos, cos, sin, sin], axis=-1)


def kernel(x_prompt, x_sample, cache_kv_latent, cache_k_rope, state_pool, page_table,
           g_ffn1_pre, w1_gate, w1_up, w1_down, g_ffn1_post,
           g_mix_pre, w_in, w_pool, pool_scale, g_q, w_uq, g_kv, w_uk, w_uv, w_out, g_mix_post,
           g_ffn2_pre, w2_gate, w2_up, w2_down, g_ffn2_post):
    depth = w_in.shape[0]
    bp, sp, d = x_prompt.shape
    bs, ts, _ = x_sample.shape
    n_pages = page_table.shape[1]
    page = cache_kv_latent.shape[2]
    past_len = n_pages * page
    kv_lora, n_heads, _ = w_uk.shape[1:]
    q_lora = w_uq.shape[1]
    d_pool = w_pool.shape[1] * w_pool.shape[2]
    assert ts == 1, "the decode kernel handles one new token per sample sequence"
    sm_scale = (QK_NOPE + QK_ROPE) ** -0.5

    cs_p = _rope_table(jnp.arange(sp, dtype=jnp.int32))
    cs_s = jnp.tile(_rope_table(past_len + jnp.arange(ts, dtype=jnp.int32)), (bs, 1))

    hp = x_prompt.reshape(bp * sp, d)
    hs = x_sample.reshape(bs * ts, d)
    outs = [[] for _ in range(6)]
    for l in range(depth):
        row = lambda g: g[l][None, :]
        w_in_t = jnp.swapaxes(w_in[l], 0, 1)
        w_rope_t = jnp.swapaxes(_rotate_half_columns(w_in[l][:, d_pool + q_lora + kv_lora:]), 0, 1)
        w_in2 = jnp.concatenate([w_in_t, w_rope_t], axis=0).astype(BF16)
        wq = w_uq[l].reshape(q_lora, n_heads, QK_NOPE + QK_ROPE)
        w_q2 = jnp.concatenate(
            [wq, _rotate_half_columns(wq[..., QK_NOPE:])], axis=-1).reshape(q_lora, n_heads * HEAD_W).astype(BF16)
        w_kv2 = jnp.concatenate(
            [w_uk[l].reshape(kv_lora, -1), w_uv[l].reshape(kv_lora, -1)], axis=1).astype(BF16)
        w_uk_t = jnp.transpose(w_uk[l], (1, 2, 0)).astype(BF16)
        w_uv_h = jnp.transpose(w_uv[l], (1, 0, 2)).astype(BF16)
        w_pool_l = w_pool[l].astype(BF16)
        w_out_l = w_out[l].astype(BF16)

        hp, hs = _ffn(hp, hs, row(g_ffn1_pre), w1_gate[l], w1_up[l], w1_down[l], row(g_ffn1_post))
        (u_p, ckv_p, kpe_p, q_p, k_p, v_p, pool_p, u_s, ckv_s, kpe_s, qlat_s, qpe_s) = _proj(
            hp, hs, cs_p, cs_s, row(g_mix_pre), w_in2, row(g_q), w_q2, row(g_kv), w_kv2, w_uk_t,
            w_pool_l, row(pool_scale), seq=sp, n_heads=n_heads)
        attn_p = _attn(q_p, k_p, v_p, batch=bp, seq=sp, sm_scale=sm_scale)
        o_lat = _decode(page_table, jnp.transpose(qlat_s, (1, 0, 2)), jnp.transpose(qpe_s, (1, 0, 2)),
                        ckv_s, kpe_s, cache_kv_latent[l],
                        jnp.swapaxes(cache_k_rope[l], 1, 2),
                        sm_scale=sm_scale)
        state_t = jnp.transpose(state_pool[l], (1, 0, 2))
        hp, hs = _outproj(hp, pool_p, attn_p, hs, jnp.transpose(o_lat, (1, 0, 2)), w_uv_h, state_t, u_s,
                          w_pool_l, row(pool_scale), w_out_l, row(g_mix_post), past_len=past_len)
        hp, hs = _ffn(hp, hs, row(g_ffn2_pre), w2_gate[l], w2_up[l], w2_down[l], row(g_ffn2_post))

        u_p3 = u_p.reshape(bp, sp, d_pool)
        outs[0].append(ckv_p.reshape(bp, sp, kv_lora))
        outs[1].append(jnp.swapaxes(kpe_p, 1, 2))
        outs[2].append(u_p3[:, sp - POOL_STATE:])
        outs[3].append(ckv_s.reshape(bs, ts, kv_lora))
        outs[4].append(kpe_s.reshape(bs, ts, QK_ROPE))
        new_rows_t = jnp.transpose(u_s.reshape(bs, ts, d_pool), (1, 0, 2))
        outs[5].append(jnp.transpose(jnp.concatenate([state_t[ts:], new_rows_t], axis=0), (1, 0, 2)))
    return (hp.reshape(bp, sp, d), hs.reshape(bs, ts, d), *[jnp.stack(o) for o in outs])
```

```python
import functools
import math

import jax
import jax.numpy as jnp
from jax import lax
from jax.experimental import pallas as pl
from jax.experimental.pallas import tpu as pltpu

F32 = jnp.float32
BF16 = jnp.bfloat16

EPS = 1e-6
ROPE_THETA = 10000.0
POOL_WINDOWS = (2, 4, 8, 16)
POOL_STATE = max(POOL_WINDOWS) - 1
QK_NOPE = 128
QK_ROPE = 64
V_HEAD = 128
HEAD_W = 2 * 128

V7X_VMEM_BYTES = 64 * 1024 * 1024
VMEM_LIMIT = V7X_VMEM_BYTES - 8 * 1024 * 1024

ROW_TILE = 512
FFN_ROW_TILE = 1024
FF_TILE = 256
NORM_ROWS = 128
PROJ_TILE = 512
ATTN_TILE = 512
ATTN_HEADS_IN_FLIGHT = 4
DECODE_CHUNK_PAGES = 32


def _params(n_grid_dims):
    return pltpu.CompilerParams(
        dimension_semantics=("arbitrary",) * n_grid_dims, vmem_limit_bytes=VMEM_LIMIT)


def _rms(x, g):
    return x * lax.rsqrt(jnp.mean(x * x, axis=-1, keepdims=True) + EPS) * g


def _dot(a, b):
    return jnp.dot(a, b, preferred_element_type=F32)


def _dot_nt(a, b):
    return lax.dot_general(a, b, (((1,), (1,)), ((), ())), preferred_element_type=F32)


def _for_row_chunks(rows, fn):
    n_chunks, rem = divmod(rows, NORM_ROWS)
    assert rem == 0

    def step(c, carry):
        fn(pl.ds(pl.multiple_of(c * NORM_ROWS, NORM_ROWS), NORM_ROWS))
        return carry

    lax.fori_loop(0, n_chunks, step, 0, unroll=2 if n_chunks % 2 == 0 else 1)


def _row_inv_rms(x):
    return lax.rsqrt(jnp.mean(x * x, axis=-1, keepdims=True) + EPS)


def _ffn_kernel(xp_ref, xs_ref, gpre_ref, wg_hbm, wu_hbm, wd_hbm, gpost_ref,
                yp_ref, ys_ref, xnp_ref, xns_ref, wg_buf, wu_buf, wd_buf, w_sem, *, n_ff_tiles):
    row_tile = pl.program_id(0)
    last_tile = row_tile == pl.num_programs(0) - 1
    tm = xp_ref.shape[0]
    n_pairs = n_ff_tiles // 2

    def weight_copies(j, slot):
        cols = pl.ds(pl.multiple_of(j * FF_TILE, FF_TILE), FF_TILE)
        return (pltpu.make_async_copy(wg_hbm.at[:, cols], wg_buf.at[slot], w_sem.at[0, slot]),
                pltpu.make_async_copy(wu_hbm.at[:, cols], wu_buf.at[slot], w_sem.at[1, slot]),
                pltpu.make_async_copy(wd_hbm.at[cols, :], wd_buf.at[slot], w_sem.at[2, slot]))

    def start(j, slot):
        for c in weight_copies(j, slot):
            c.start(priority=1)

    def wait(j, slot):
        for c in weight_copies(j, slot):
            c.wait()

    def prologue(x_ref, y_ref, xn_ref):
        def chunk(rs):
            inv = _row_inv_rms(x_ref[rs, :])
            xn_ref[rs, :] = (x_ref[rs, :] * inv * gpre_ref[...]).astype(BF16)
            y_ref[rs, :] = jnp.zeros((NORM_ROWS, y_ref.shape[1]), F32)

        _for_row_chunks(x_ref.shape[0], chunk)

    def epilogue(x_ref, y_ref):
        half_gain = 0.5 * gpost_ref[...]

        def chunk(rs):
            inv = _row_inv_rms(y_ref[rs, :])
            y_ref[rs, :] = x_ref[rs, :] + y_ref[rs, :] * inv * half_gain

        _for_row_chunks(x_ref.shape[0], chunk)

    def swiglu_part(xn, slot):
        gate = _dot(xn, wg_buf[slot].astype(BF16))
        up = _dot(xn, wu_buf[slot].astype(BF16))
        hidden = (gate / (1.0 + jnp.exp(-gate))) * up
        return _dot(hidden.astype(BF16), wd_buf[slot].astype(BF16))

    def accumulate(slot, with_sample):
        if with_sample:
            part = swiglu_part(jnp.concatenate([xnp_ref[...], xns_ref[...]], axis=0), slot)
            yp_ref[...] += part[:tm]
            ys_ref[...] += part[tm:]
        else:
            yp_ref[...] += swiglu_part(xnp_ref[...], slot)

    def ff_loop(with_sample):
        def pair(jj, carry):
            j = 2 * jj
            start(j + 1, 1)
            wait(j, 0)
            accumulate(0, with_sample)
            if with_sample:
                pl.when(jj < n_pairs - 1)(lambda: start(j + 2, 0))
            else:
                start(lax.rem(j + 2, n_ff_tiles), 0)
            wait(j + 1, 1)
            accumulate(1, with_sample)
            return carry

        lax.fori_loop(0, n_pairs, pair, 0)

    pl.when(row_tile == 0)(lambda: start(0, 0))
    prologue(xp_ref, yp_ref, xnp_ref)
    pl.when(last_tile)(lambda: prologue(xs_ref, ys_ref, xns_ref))
    pl.when(jnp.logical_not(last_tile))(lambda: ff_loop(False))
    pl.when(last_tile)(lambda: ff_loop(True))
    epilogue(xp_ref, yp_ref)
    pl.when(last_tile)(lambda: epilogue(xs_ref, ys_ref))


def _ffn(xp, xs, g_pre, w_gate, w_up, w_down, g_post):
    mp, d = xp.shape
    ms = xs.shape[0]
    d_ff = w_gate.shape[1]
    tm = FFN_ROW_TILE
    nj = d_ff // FF_TILE
    assert mp % tm == 0 and d_ff % FF_TILE == 0 and nj % 2 == 0
    prompt_row = lambda i: (i, 0)
    fixed = lambda i: (0, 0)
    in_hbm = pl.BlockSpec(memory_space=pl.ANY)
    return pl.pallas_call(
        functools.partial(_ffn_kernel, n_ff_tiles=nj),
        out_shape=(jax.ShapeDtypeStruct((mp, d), F32), jax.ShapeDtypeStruct((ms, d), F32)),
        grid=(mp // tm,),
        in_specs=[
            pl.BlockSpec((tm, d), prompt_row),
            pl.BlockSpec((ms, d), fixed),
            pl.BlockSpec((1, d), fixed),
            in_hbm, in_hbm, in_hbm,
            pl.BlockSpec((1, d), fixed),
        ],
        out_specs=(pl.BlockSpec((tm, d), prompt_row), pl.BlockSpec((ms, d), fixed)),
        scratch_shapes=[
            pltpu.VMEM((tm, d), BF16), pltpu.VMEM((ms, d), BF16),
            pltpu.VMEM((2, d, FF_TILE), F32), pltpu.VMEM((2, d, FF_TILE), F32),
            pltpu.VMEM((2, FF_TILE, d), F32),
            pltpu.SemaphoreType.DMA((3, 2)),
        ],
        compiler_params=_params(1),
        name="ffn",
    )(xp, xs, g_pre, w_gate, w_up, w_down, g_post)


def _rope_pairs(slab, cs):
    t = slab * cs
    return t + pltpu.roll(t, QK_ROPE, axis=1)


def _proj_kernel(hp_ref, hs_ref, csp_ref, css_ref, gmix_ref, win_ref, gq_ref, wq_ref, gkv_ref,
                 wkv_ref, wukt_ref, wp_ref, pscale_ref,
                 up_ref, ckvp_ref, kpep_ref, q_ref, k_ref, v_ref, poolp_ref,
                 us_ref, ckvs_ref, kpes_ref, qlat_ref, qpes_ref, ext_ref,
                 *, n_prompt_tiles, tiles_per_seq, n_heads, d_pool, q_lora, kv_lora):
    i = pl.program_id(0)

    @pl.when(i == 0)
    def _():
        ext_ref[...] = jnp.zeros(ext_ref.shape, F32)

    def common(h_ref, cs_ref, u_ref, ckv_ref):
        hn = _rms(h_ref[...], gmix_ref[...]).astype(BF16)
        z = _dot_nt(hn, win_ref[...])
        u = z[:, :d_pool]
        u_ref[...] = u
        cqn = _rms(z[:, d_pool:d_pool + q_lora], gq_ref[...]).astype(BF16)
        ckv = _rms(z[:, d_pool + q_lora:d_pool + q_lora + kv_lora], gkv_ref[...])
        ckv_ref[...] = ckv
        cs = cs_ref[...]
        kpe = _rope_pairs(z[:, d_pool + q_lora + kv_lora:], cs)
        qq = _dot(cqn, wq_ref[...])
        return u, ckv, kpe, qq, cs

    @pl.when(i < n_prompt_tiles)
    def _():
        u, ckv, kpe, qq, cs = common(hp_ref, csp_ref, up_ref, ckvp_ref)
        kpep_ref[0] = kpe.T[:QK_ROPE, :]
        _pool_rows(u, lax.rem(i, tiles_per_seq), wp_ref, pscale_ref, poolp_ref, ext_ref)
        lane = lax.broadcasted_iota(jnp.int32, kpe.shape, 1)
        kpe_pad = jnp.where(lane < QK_ROPE, kpe, 0.0).astype(BF16)
        kv = _dot(ckv.astype(BF16), wkv_ref[...])
        for h in range(n_heads):
            v_lo = n_heads * QK_NOPE + h * V_HEAD
            v_ref[h] = kv[:, v_lo:v_lo + V_HEAD].astype(BF16)
            slab = qq[:, h * HEAD_W:(h + 1) * HEAD_W]
            q_ref[h, :, :QK_NOPE] = slab[:, :QK_NOPE].astype(BF16)
            q_ref[h, :, QK_NOPE:] = _rope_pairs(slab[:, QK_NOPE:], cs).astype(BF16)
            k_ref[h, :, :QK_NOPE] = kv[:, h * QK_NOPE:(h + 1) * QK_NOPE].astype(BF16)
            k_ref[h, :, QK_NOPE:] = kpe_pad

    @pl.when(i == n_prompt_tiles)
    def _():
        _, _, kpe, qq, cs = common(hs_ref, css_ref, us_ref, ckvs_ref)
        kpes_ref[...] = kpe[:, :QK_ROPE]
        for h in range(n_heads):
            slab = qq[:, h * HEAD_W:(h + 1) * HEAD_W]
            qlat_ref[h] = _dot(slab[:, :QK_NOPE].astype(BF16), wukt_ref[h]).astype(BF16)
            qpes_ref[h] = _rope_pairs(slab[:, QK_NOPE:], cs)[:, :QK_ROPE].astype(BF16)


def _proj(hp, hs, cs_p, cs_s, g_mix, w_in2, g_q, w_q2, g_kv, w_kv2, w_uk_t, w_pool, pool_scale, *, seq, n_heads):
    mp, d = hp.shape
    ms = hs.shape[0]
    tm = PROJ_TILE
    ntp = mp // tm
    tiles_per_seq = seq // tm
    assert mp % tm == 0 and seq % tm == 0
    d_in2 = w_in2.shape[0]
    q_lora = w_q2.shape[0]
    kv_lora = w_kv2.shape[0]
    d_pool = d_in2 - q_lora - kv_lora - 2 * QK_ROPE
    prompt_row = lambda i: (jnp.minimum(i, ntp - 1), 0)
    prompt_head_row = lambda i: (0, jnp.minimum(i, ntp - 1), 0)
    fixed2 = lambda i: (0, 0)
    fixed3 = lambda i: (0, 0, 0)
    whole = lambda a: pl.BlockSpec(a.shape, fixed2 if a.ndim == 2 else fixed3)
    out_shape = (
        jax.ShapeDtypeStruct((mp, d_pool), F32),
        jax.ShapeDtypeStruct((mp, kv_lora), F32),
        jax.ShapeDtypeStruct((mp // seq, QK_ROPE, seq), F32),
        jax.ShapeDtypeStruct((n_heads, mp, HEAD_W), BF16),
        jax.ShapeDtypeStruct((n_heads, mp, HEAD_W), BF16),
        jax.ShapeDtypeStruct((n_heads, mp, V_HEAD), BF16),
        jax.ShapeDtypeStruct((mp, d_pool), BF16),
        jax.ShapeDtypeStruct((ms, d_pool), F32),
        jax.ShapeDtypeStruct((ms, kv_lora), F32),
        jax.ShapeDtypeStruct((ms, QK_ROPE), F32),
        jax.ShapeDtypeStruct((n_heads, ms, kv_lora), BF16),
        jax.ShapeDtypeStruct((n_heads, ms, QK_ROPE), BF16),
    )
    out_specs = (
        pl.BlockSpec((tm, d_pool), prompt_row),
        pl.BlockSpec((tm, kv_lora), prompt_row),
        pl.BlockSpec((1, QK_ROPE, tm), lambda i: (jnp.minimum(i, ntp - 1) // tiles_per_seq, 0,
                                                  jnp.minimum(i, ntp - 1) % tiles_per_seq)),
        pl.BlockSpec((n_heads, tm, HEAD_W), prompt_head_row),
        pl.BlockSpec((n_heads, tm, HEAD_W), prompt_head_row),
        pl.BlockSpec((n_heads, tm, V_HEAD), prompt_head_row),
        pl.BlockSpec((tm, d_pool), prompt_row),
        pl.BlockSpec((ms, d_pool), fixed2),
        pl.BlockSpec((ms, kv_lora), fixed2),
        pl.BlockSpec((ms, QK_ROPE), fixed2),
        pl.BlockSpec((n_heads, ms, kv_lora), fixed3),
        pl.BlockSpec((n_heads, ms, QK_ROPE), fixed3),
    )
    return pl.pallas_call(
        functools.partial(_proj_kernel, n_prompt_tiles=ntp, tiles_per_seq=tiles_per_seq, n_heads=n_heads,
                          d_pool=d_pool, q_lora=q_lora, kv_lora=kv_lora),
        out_shape=out_shape,
        grid=(ntp + 1,),
        in_specs=[
            pl.BlockSpec((tm, d), prompt_row),
            pl.BlockSpec((ms, d), fixed2),
            pl.BlockSpec((tm, 2 * QK_ROPE), lambda i: (jnp.minimum(i, ntp - 1) % tiles_per_seq, 0)),
            whole(cs_s), whole(g_mix), whole(w_in2), whole(g_q), whole(w_q2), whole(g_kv),
            whole(w_kv2), whole(w_uk_t), whole(w_pool), whole(pool_scale),
        ],
        out_specs=out_specs,
        scratch_shapes=[pltpu.VMEM((tm + POOL_STATE + 1, d_pool), F32)],
        compiler_params=_params(1),
        name="proj",
    )(hp, hs, cs_p, cs_s, g_mix, w_in2, g_q, w_q2, g_kv, w_kv2, w_uk_t, w_pool, pool_scale)


def _pool_delta(window_sum, cur, inv_cnt):
    return window_sum * inv_cnt - cur


def _pool_rows(u, seq_tile, wp_ref, scale_ref, o_ref, ext_ref):
    rows, d_pool = u.shape
    group = d_pool // len(POOL_WINDOWS)
    halo = POOL_STATE + 1

    ext_ref[:halo, :] = jnp.where(seq_tile == 0, 0.0, ext_ref[rows:rows + halo, :])
    ext_ref[halo:, :] = u
    pos = seq_tile * rows + lax.broadcasted_iota(jnp.int32, (rows, 1), 0)
    for g, w in enumerate(POOL_WINDOWS):
        cols = slice(g * group, (g + 1) * group)
        cur = ext_ref[halo:, cols]
        acc = cur
        for k in range(1, w):
            acc = acc + ext_ref[halo - k:halo - k + rows, cols]
        inv_cnt = 1.0 / jnp.minimum(pos + 1, w).astype(F32)
        delta = _pool_delta(acc, cur, inv_cnt)
        out = _dot(delta.astype(BF16), wp_ref[g])
        o_ref[:, cols] = (out * scale_ref[:, cols]).astype(BF16)


def _attn_kernel(q_ref, k_ref, v_ref, o_ref, *, n_heads, n_q_tiles, exp2_scale):
    qi = pl.program_id(1)
    t = ATTN_TILE
    row = lax.broadcasted_iota(jnp.int32, (t, t), 0)
    col = lax.broadcasted_iota(jnp.int32, (t, t), 1)
    causal = row >= col

    def tile(n_past):
        def head(h, carry):
            q = q_ref[h]
            s_diag = jnp.where(causal, _dot_nt(q, k_ref[h, n_past:n_past + t, :]), -jnp.inf)
            m = jnp.max(s_diag, axis=1, keepdims=True)
            if n_past:
                s_past = _dot_nt(q, k_ref[h, :n_past, :])
                m = jnp.maximum(m, jnp.max(s_past, axis=1, keepdims=True))
            p_diag = jnp.exp2((s_diag - m) * exp2_scale)
            l = jnp.sum(p_diag, axis=1, keepdims=True)
            o = _dot(p_diag.astype(BF16), v_ref[h, n_past:n_past + t, :])
            if n_past:
                p_past = jnp.exp2((s_past - m) * exp2_scale)
                l = l + jnp.sum(p_past, axis=1, keepdims=True)
                o = o + _dot(p_past.astype(BF16), v_ref[h, :n_past, :])
            o_ref[h] = (o / l).astype(BF16)
            return carry

        lax.fori_loop(0, n_heads, head, 0, unroll=ATTN_HEADS_IN_FLIGHT)

    for kq in range(n_q_tiles):
        pl.when(qi == kq)(functools.partial(tile, kq * t))


def _attn(q, k, v, *, batch, seq, sm_scale):
    n_heads, m, _ = q.shape
    t = ATTN_TILE
    nq = seq // t
    assert seq % t == 0
    return pl.pallas_call(
        functools.partial(_attn_kernel, n_heads=n_heads, n_q_tiles=nq,
                          exp2_scale=sm_scale * math.log2(math.e)),
        out_shape=jax.ShapeDtypeStruct((n_heads, m, V_HEAD), BF16),
        grid=(batch, nq),
        in_specs=[
            pl.BlockSpec((n_heads, t, HEAD_W), lambda b, i: (0, b * nq + i, 0)),
            pl.BlockSpec((n_heads, seq, HEAD_W), lambda b, i: (0, b, 0)),
            pl.BlockSpec((n_heads, seq, V_HEAD), lambda b, i: (0, b, 0)),
        ],
        out_specs=pl.BlockSpec((n_heads, t, V_HEAD), lambda b, i: (0, b * nq + i, 0)),
        compiler_params=_params(2),
        name="attn",
    )(q, k, v)


def _decode_kernel(pt_ref, qlat_ref, qpe_ref, ckvs_ref, kpes_ref, kv_hbm, rope_hbm, o_ref,
                   kv_buf, rope_buf, kv_sem, rope_sem, *, n_pages, sm_scale):
    b = pl.program_id(0)
    n_samples = pl.num_programs(0)
    slot = b % 2

    def page_copies(sample, sl):
        copies = []
        for i in range(n_pages):
            page_id = pt_ref[sample * n_pages + i]
            copies.append(pltpu.make_async_copy(kv_hbm.at[page_id], kv_buf.at[sl, i], kv_sem.at[sl]))
            copies.append(pltpu.make_async_copy(rope_hbm.at[page_id], rope_buf.at[sl, i], rope_sem.at[sl]))
        return copies

    @pl.when(b == 0)
    def _():
        for c in page_copies(0, 0):
            c.start()

    @pl.when(b + 1 < n_samples)
    def _():
        for c in page_copies(b + 1, 1 - slot):
            c.start()

    for c in page_copies(b, slot):
        c.wait()

    qlat = qlat_ref[b]
    qpe = qpe_ref[b]
    own_ckv = ckvs_ref[pl.ds(b, 1), :].astype(BF16).astype(F32)
    own_kpe = kpes_ref[pl.ds(b, 1), :].astype(BF16).astype(F32)
    s_own = (jnp.sum(qlat.astype(F32) * own_ckv, axis=1, keepdims=True)
             + jnp.sum(qpe.astype(F32) * own_kpe, axis=1, keepdims=True)) * sm_scale

    chunk = DECODE_CHUNK_PAGES
    ckv_parts, s_parts = [], []
    for c0 in range(0, n_pages, chunk):
        ckv = jnp.concatenate([kv_buf[slot, i] for i in range(c0, c0 + chunk)], axis=0).astype(BF16)
        kpe_t = jnp.concatenate([rope_buf[slot, i] for i in range(c0, c0 + chunk)], axis=1).astype(BF16)
        ckv_parts.append(ckv)
        s_parts.append((_dot_nt(qlat, ckv) + _dot(qpe, kpe_t)) * sm_scale)
    m = s_own
    for s in s_parts:
        m = jnp.maximum(m, jnp.max(s, axis=1, keepdims=True))
    p_own = jnp.exp(s_own - m)
    l = p_own
    acc = p_own.astype(BF16).astype(F32) * own_ckv
    for s, ckv in zip(s_parts, ckv_parts):
        p = jnp.exp(s - m)
        l = l + jnp.sum(p, axis=1, keepdims=True)
        acc = acc + _dot(p.astype(BF16), ckv)
    o_ref[b] = acc / l


def _decode(page_table, q_lat, q_pe, ckv_own, kpe_own, cache_kv, cache_rope_t, *, sm_scale):
    bs, n_heads, kv_lora = q_lat.shape
    n_pages = page_table.shape[1]
    page = cache_kv.shape[1]
    assert n_pages % DECODE_CHUNK_PAGES == 0
    whole = lambda a: pl.BlockSpec(a.shape, lambda b, pt, nd=a.ndim: (0,) * nd)
    grid_spec = pltpu.PrefetchScalarGridSpec(
        num_scalar_prefetch=1,
        grid=(bs,),
        in_specs=[whole(q_lat), whole(q_pe), whole(ckv_own), whole(kpe_own),
                  pl.BlockSpec(memory_space=pl.ANY), pl.BlockSpec(memory_space=pl.ANY)],
        out_specs=pl.BlockSpec((bs, n_heads, kv_lora), lambda b, pt: (0, 0, 0)),
        scratch_shapes=[
            pltpu.VMEM((2, n_pages, page, kv_lora), F32),
            pltpu.VMEM((2, n_pages, QK_ROPE, page), F32),
            pltpu.SemaphoreType.DMA((2,)),
            pltpu.SemaphoreType.DMA((2,)),
        ],
    )
    return pl.pallas_call(
        functools.partial(_decode_kernel, n_pages=n_pages, sm_scale=sm_scale),
        out_shape=jax.ShapeDtypeStruct((bs, n_heads, kv_lora), F32),
        grid_spec=grid_spec,
        compiler_params=_params(1),
        name="decode",
    )(page_table.reshape(-1), q_lat, q_pe, ckv_own, kpe_own, cache_kv, cache_rope_t)


def _sample_mixers(olat_ref, wuv_ref, sp_ref, us_ref, wp_ref, scale_ref, *, past_len):
    n_heads = olat_ref.shape[0]
    group = us_ref.shape[1] // len(POOL_WINDOWS)
    attn = jnp.concatenate(
        [_dot(olat_ref[h].astype(BF16), wuv_ref[h]).astype(BF16) for h in range(n_heads)], axis=1)
    pooled = []
    for g, w in enumerate(POOL_WINDOWS):
        cols = slice(g * group, (g + 1) * group)
        cur = us_ref[:, cols]
        acc = cur
        for k in range(1, w):
            acc = acc + sp_ref[POOL_STATE - k, :, cols]
        delta = _pool_delta(acc, cur, 1.0 / min(past_len + 1, w))
        out = _dot(delta.astype(BF16), wp_ref[g])
        pooled.append((out * scale_ref[:, cols]).astype(BF16))
    return jnp.concatenate(pooled, axis=1), attn


def _outproj_kernel(hp_ref, poolp_ref, attnp_ref, hs_ref, olat_ref, wuv_ref, sp_ref, us_ref, wp_ref,
                    scale_ref, wo_ref, g_ref, yp_ref, ys_ref, *, n_prompt_tiles, d_pool, past_len):
    i = pl.program_id(0)

    def body(h_ref, pool, attn, y_ref):
        mix = _dot(pool, wo_ref[:d_pool, :]) + _dot(attn, wo_ref[d_pool:, :])
        y_ref[...] = h_ref[...] + _rms(mix, g_ref[...])

    @pl.when(i < n_prompt_tiles)
    def _():
        heads = [attnp_ref[h] for h in range(attnp_ref.shape[0])]
        body(hp_ref, poolp_ref[...], jnp.concatenate(heads, axis=1), yp_ref)

    @pl.when(i == n_prompt_tiles)
    def _():
        pool, attn = _sample_mixers(olat_ref, wuv_ref, sp_ref, us_ref, wp_ref, scale_ref, past_len=past_len)
        body(hs_ref, pool, attn, ys_ref)


def _outproj(hp, pool_p, attn_p, hs, o_lat, w_uv_h, state_t, u_s, w_pool, pool_scale, w_out, g_post,
             *, past_len):
    mp, d = hp.shape
    ms = hs.shape[0]
    d_pool = pool_p.shape[1]
    n_heads, _, v_head = attn_p.shape
    tm = ROW_TILE
    ntp = mp // tm
    prompt_row = lambda i: (jnp.minimum(i, ntp - 1), 0)
    whole = lambda a: pl.BlockSpec(a.shape, lambda i, nd=a.ndim: (0,) * nd)
    return pl.pallas_call(
        functools.partial(_outproj_kernel, n_prompt_tiles=ntp, d_pool=d_pool, past_len=past_len),
        out_shape=(jax.ShapeDtypeStruct((mp, d), F32), jax.ShapeDtypeStruct((ms, d), F32)),
        grid=(ntp + 1,),
        in_specs=[
            pl.BlockSpec((tm, d), prompt_row),
            pl.BlockSpec((tm, d_pool), prompt_row),
            pl.BlockSpec((n_heads, tm, v_head), lambda i: (0, jnp.minimum(i, ntp - 1), 0)),
            whole(hs), whole(o_lat), whole(w_uv_h), whole(state_t), whole(u_s), whole(w_pool),
            whole(pool_scale), whole(w_out), whole(g_post),
        ],
        out_specs=(pl.BlockSpec((tm, d), prompt_row), pl.BlockSpec((ms, d), lambda i: (0, 0))),
        compiler_params=_params(1),
        name="outproj",
    )(hp, pool_p, attn_p, hs, o_lat, w_uv_h, state_t, u_s, w_pool, pool_scale, w_out, g_post)


def _rotate_half_columns(w):
    half = QK_ROPE // 2
    return jnp.concatenate([-w[..., half:], w[..., :half]], axis=-1)


def _rope_table(pos):
    half = QK_ROPE // 2
    inv = ROPE_THETA ** (-jnp.arange(half, dtype=F32) / half)
    ang = pos.astype(F32)[:, None] * inv[None, :]
    cos, sin = jnp.cos(ang), jnp.sin(ang)
    return jnp.concatenate([cos, cos, sin, sin], axis=-1)


def kernel(x_prompt, x_sample, cache_kv_latent, cache_k_rope, state_pool, page_table,
           g_ffn1_pre, w1_gate, w1_up, w1_down, g_ffn1_post,
           g_mix_pre, w_in, w_pool, pool_scale, g_q, w_uq, g_kv, w_uk, w_uv, w_out, g_mix_post,
           g_ffn2_pre, w2_gate, w2_up, w2_down, g_ffn2_post):
    depth = w_in.shape[0]
    bp, sp, d = x_prompt.shape
    bs, ts, _ = x_sample.shape
    n_pages = page_table.shape[1]
    page = cache_kv_latent.shape[2]
    past_len = n_pages * page
    kv_lora, n_heads, _ = w_uk.shape[1:]
    q_lora = w_uq.shape[1]
    d_pool = w_pool.shape[1] * w_pool.shape[2]
    assert ts == 1, "the decode kernel handles one new token per sample sequence"
    sm_scale = (QK_NOPE + QK_ROPE) ** -0.5

    cs_p = _rope_table(jnp.arange(sp, dtype=jnp.int32))
    cs_s = jnp.tile(_rope_table(past_len + jnp.arange(ts, dtype=jnp.int32)), (bs, 1))

    hp = x_prompt.reshape(bp * sp, d)
    hs = x_sample.reshape(bs * ts, d)
    outs = [[] for _ in range(6)]
    for l in range(depth):
        row = lambda g: g[l][None, :]
        w_in_t = jnp.swapaxes(w_in[l], 0, 1)
        w_rope_t = jnp.swapaxes(_rotate_half_columns(w_in[l][:, d_pool + q_lora + kv_lora:]), 0, 1)
        w_in2 = jnp.concatenate([w_in_t, w_rope_t], axis=0).astype(BF16)
        wq = w_uq[l].reshape(q_lora, n_heads, QK_NOPE + QK_ROPE)
        w_q2 = jnp.concatenate(
            [wq, _rotate_half_columns(wq[..., QK_NOPE:])], axis=-1).reshape(q_lora, n_heads * HEAD_W).astype(BF16)
        w_kv2 = jnp.concatenate(
            [w_uk[l].reshape(kv_lora, -1), w_uv[l].reshape(kv_lora, -1)], axis=1).astype(BF16)
        w_uk_t = jnp.transpose(w_uk[l], (1, 2, 0)).astype(BF16)
        w_uv_h = jnp.transpose(w_uv[l], (1, 0, 2)).astype(BF16)
        w_pool_l = w_pool[l].astype(BF16)
        w_out_l = w_out[l].astype(BF16)

        hp, hs = _ffn(hp, hs, row(g_ffn1_pre), w1_gate[l], w1_up[l], w1_down[l], row(g_ffn1_post))
        (u_p, ckv_p, kpe_p, q_p, k_p, v_p, pool_p, u_s, ckv_s, kpe_s, qlat_s, qpe_s) = _proj(
            hp, hs, cs_p, cs_s, row(g_mix_pre), w_in2, row(g_q), w_q2, row(g_kv), w_kv2, w_uk_t,
            w_pool_l, row(pool_scale), seq=sp, n_heads=n_heads)
        attn_p = _attn(q_p, k_p, v_p, batch=bp, seq=sp, sm_scale=sm_scale)
        o_lat = _decode(page_table, jnp.transpose(qlat_s, (1, 0, 2)), jnp.transpose(qpe_s, (1, 0, 2)),
                        ckv_s, kpe_s, cache_kv_latent[l],
                        jnp.swapaxes(cache_k_rope[l], 1, 2),
                        sm_scale=sm_scale)
        state_t = jnp.transpose(state_pool[l], (1, 0, 2))
        hp, hs = _outproj(hp, pool_p, attn_p, hs, jnp.transpose(o_lat, (1, 0, 2)), w_uv_h, state_t, u_s,
                          w_pool_l, row(pool_scale), w_out_l, row(g_mix_post), past_len=past_len)
        hp, hs = _ffn(hp, hs, row(g_ffn2_pre), w2_gate[l], w2_up[l], w2_down[l], row(g_ffn2_post))

        u_p3 = u_p.reshape(bp, sp, d_pool)
        outs[0].append(ckv_p.reshape(bp, sp, kv_lora))
        outs[1].append(jnp.swapaxes(kpe_p, 1, 2))
        outs[2].append(u_p3[:, sp - POOL_STATE:])
        outs[3].append(ckv_s.reshape(bs, ts, kv_lora))
        outs[4].append(kpe_s.reshape(bs, ts, QK_ROPE))
        new_rows_t = jnp.transpose(u_s.reshape(bs, ts, d_pool), (1, 0, 2))
        outs[5].append(jnp.transpose(jnp.concatenate([state_t[ts:], new_rows_t], axis=0), (1, 0, 2)))
    return (hp.reshape(bp, sp, d), hs.reshape(bs, ts, d), *[jnp.stack(o) for o in outs])
```
